```python
import numpy as np
import jax
import jax.numpy as jnp
from jax import lax

D_MODEL = 1024
BATCH = 8
SEQ = 4096
DEPTH = 4

N_HEADS = 16
HEAD_DIM = D_MODEL // N_HEADS
ROPE_THETA = 500000.0
ROPE_DIM = HEAD_DIM // 4
D_FF = 4 * D_MODEL
NORM_EPS = 1e-6
N_MIXERS = 3
N_LAYERS_A = (DEPTH + 2) // N_MIXERS
N_LAYERS_B = (DEPTH + 1) // N_MIXERS
N_LAYERS_C = DEPTH // N_MIXERS
SB_BLOCK = 128
NSA_KV_GROUPS = 4
NSA_Q_PER_GROUP = N_HEADS // NSA_KV_GROUPS
KV_WIDTH = NSA_KV_GROUPS * HEAD_DIM
NSA_IN = D_MODEL + 6 * KV_WIDTH + 3 * N_HEADS
CMP_LEN = 32
CMP_STRIDE = 16
CMP_HIDDEN = 4 * HEAD_DIM
SLC_LEN = 64
N_SELECT = 16
WINDOW = 512
NSA_CHUNK = 32
FORCE_BONUS = 1e4
NEG_INF = -1e30
CONV_WIDTH = 31

kernel_name = 'hybrid_stickbreak_nsa_conformer_trunk'


def rms_norm(x, g):
    xf = x.astype(jnp.float32)
    y = xf * lax.rsqrt(jnp.mean(xf * xf, axis=-1, keepdims=True) + NORM_EPS)
    return (y * g.astype(jnp.float32)).astype(x.dtype)


def layer_norm(x, g, b):
    xf = x.astype(jnp.float32)
    mu = jnp.mean(xf, axis=-1, keepdims=True)
    var = jnp.mean(jnp.square(xf - mu), axis=-1, keepdims=True)
    y = (xf - mu) * lax.rsqrt(var + NORM_EPS)
    return (y * g.astype(jnp.float32) + b.astype(jnp.float32)).astype(x.dtype)


def partial_rope(x, pos):
    half = ROPE_DIM // 2
    inv_freq = ROPE_THETA ** (-jnp.arange(half, dtype=jnp.float32) / half)
    ang = pos.astype(jnp.float32)[..., None] * inv_freq
    cos = jnp.cos(ang)[:, :, None, :]
    sin = jnp.sin(ang)[:, :, None, :]
    x1 = x[..., :half].astype(jnp.float32)
    x2 = x[..., half:ROPE_DIM].astype(jnp.float32)
    rot = jnp.concatenate([x1 * cos - x2 * sin, x2 * cos + x1 * sin], axis=-1).astype(x.dtype)
    return jnp.concatenate([rot, x[..., ROPE_DIM:]], axis=-1)


def stick_breaking_attention(u, w_in, w_out):
    B, S, _ = u.shape
    qkv = (u @ w_in).reshape(B, S, 3, N_HEADS, HEAD_DIM)
    q, k, v = qkv[:, :, 0], qkv[:, :, 1], qkv[:, :, 2]
    scale = HEAD_DIM ** -0.5
    outs = []
    for blk in range(S // SB_BLOCK):
        t0, t1 = blk * SB_BLOCK, (blk + 1) * SB_BLOCK
        z = jnp.einsum('bthd,bshd->bhts', q[:, t0:t1], k[:, :t1]).astype(jnp.float32) * scale
        t_idx = t0 + jnp.arange(SB_BLOCK)[:, None]
        s_idx = jnp.arange(t1)[None, :]
        strict = s_idx < t_idx
        log_keep = jnp.where(strict, jax.nn.log_sigmoid(-z), 0.0)
        tail = lax.cumsum(log_keep, axis=3, reverse=True) - log_keep
        a = jnp.where(strict, jnp.exp(jax.nn.log_sigmoid(z) + tail), 0.0)
        outs.append(jnp.einsum('bhts,bshd->bthd', a.astype(v.dtype), v[:, :t1]))
    o = jnp.concatenate(outs, axis=1).reshape(B, S, D_MODEL)
    return o @ w_out


def _cmp_to_slc_overlap(n_cmp, n_slc):
    c0 = np.arange(n_cmp)[:, None] * CMP_STRIDE
    s0 = np.arange(n_slc)[None, :] * SLC_LEN
    ov = np.minimum(c0 + CMP_LEN, s0 + SLC_LEN) - np.maximum(c0, s0)
    return jnp.asarray(np.maximum(ov, 0) / CMP_LEN, dtype=jnp.float32)


def _compress(x, pe, w1, w2, cmp_idx):
    B = x.shape[0]
    n_cmp = cmp_idx.shape[0]
    blocks = x[:, cmp_idx] + pe[None, None, :, None, :]
    flat = blocks.transpose(0, 1, 3, 2, 4).reshape(B, n_cmp, NSA_KV_GROUPS, CMP_LEN * HEAD_DIM)
    return jax.nn.gelu(flat @ w1) @ w2


def native_sparse_attention(u, positions, w_in, w_out, pe_k, w1_k, w2_k, pe_v, w1_v, w2_v):
    B, S, _ = u.shape
    G, R, dh = NSA_KV_GROUPS, NSA_Q_PER_GROUP, HEAD_DIM
    scale = dh ** -0.5
    proj = u @ w_in
    q = partial_rope(proj[..., :D_MODEL].reshape(B, S, N_HEADS, dh), positions)
    kv = proj[..., D_MODEL:D_MODEL + 6 * KV_WIDTH].reshape(B, S, 6, G, dh)
    gates = jax.nn.sigmoid(proj[..., D_MODEL + 6 * KV_WIDTH:].astype(jnp.float32))
    gates = gates.reshape(B, S, N_HEADS, 3).astype(u.dtype)
    k_cmp_raw, v_cmp_raw = kv[:, :, 0], kv[:, :, 1]
    k_slc = partial_rope(kv[:, :, 2], positions)
    v_slc = kv[:, :, 3]
    k_win = partial_rope(kv[:, :, 4], positions)
    v_win = kv[:, :, 5]

    n_cmp = (S - CMP_LEN) // CMP_STRIDE + 1
    cmp_idx = jnp.arange(n_cmp)[:, None] * CMP_STRIDE + jnp.arange(CMP_LEN)[None, :]
    cmp_end = jnp.arange(n_cmp) * CMP_STRIDE + CMP_LEN - 1
    k_cmp = partial_rope(_compress(k_cmp_raw, pe_k, w1_k, w2_k, cmp_idx), positions[:, cmp_end])
    v_cmp = _compress(v_cmp_raw, pe_v, w1_v, w2_v, cmp_idx)

    n_slc = S // SLC_LEN
    n_sel = min(N_SELECT, n_slc)
    overlap = _cmp_to_slc_overlap(n_cmp, n_slc)
    k_blocks = k_slc.reshape(B, n_slc, SLC_LEN, G, dh).transpose(0, 3, 1, 2, 4)
    v_blocks = v_slc.reshape(B, n_slc, SLC_LEN, G, dh).transpose(0, 3, 1, 2, 4)
    blk_ids = jnp.arange(n_slc)
    b_ix = jnp.arange(B)[:, None, None, None]
    g_ix = jnp.arange(G)[None, :, None, None]

    pad = ((0, 0), (WINDOW, 0), (0, 0), (0, 0))
    k_win_pad = jnp.pad(k_win, pad)
    v_win_pad = jnp.pad(v_win, pad)

    def chunk(ci):
        t0 = ci * NSA_CHUNK
        t_pos = t0 + jnp.arange(NSA_CHUNK)
        qc = lax.dynamic_slice_in_dim(q, t0, NSA_CHUNK, axis=1).reshape(B, NSA_CHUNK, G, R, dh)
        gc = lax.dynamic_slice_in_dim(gates, t0, NSA_CHUNK, axis=1).reshape(B, NSA_CHUNK, G, R, 3)

        sc = jnp.einsum('btgrd,bngd->bgrtn', qc, k_cmp).astype(jnp.float32) * scale
        cmp_ok = cmp_end[None, :] <= t_pos[:, None]
        any_ok = jnp.any(cmp_ok, axis=-1).astype(jnp.float32)
        p_cmp = jax.nn.softmax(jnp.where(cmp_ok, sc, NEG_INF), axis=-1) * any_ok[:, None]
        o_cmp = jnp.einsum('bgrtn,bngd->btgrd', p_cmp.astype(v_cmp.dtype), v_cmp)

        imp = jnp.einsum('bgrtn,nj->bgtj', p_cmp, overlap)
        cur = t_pos[:, None] // SLC_LEN
        forced = (blk_ids[None] == 0) | (blk_ids[None] == cur) | (blk_ids[None] == cur - 1)
        causal_blk = blk_ids[None] * SLC_LEN <= t_pos[:, None]
        score = jnp.where(causal_blk, jnp.where(forced, FORCE_BONUS, imp), NEG_INF)
        _, sel = lax.top_k(score, n_sel)
        ks = k_blocks[b_ix, g_ix, sel]
        vs = v_blocks[b_ix, g_ix, sel]
        ss = jnp.einsum('btgrd,bgtnkd->bgrtnk', qc, ks).astype(jnp.float32) * scale
        tok = sel[..., None] * SLC_LEN + jnp.arange(SLC_LEN)
        tok_ok = tok <= t_pos[None, None, :, None, None]
        ss = jnp.where(tok_ok[:, :, None], ss, NEG_INF).reshape(B, G, R, NSA_CHUNK, n_sel * SLC_LEN)
        p_slc = jax.nn.softmax(ss, axis=-1).reshape(B, G, R, NSA_CHUNK, n_sel, SLC_LEN)
        o_slc = jnp.einsum('bgrtnk,bgtnkd->btgrd', p_slc.astype(vs.dtype), vs)

        band = NSA_CHUNK + WINDOW
        kw = lax.dynamic_slice_in_dim(k_win_pad, t0, band, axis=1)
        vw = lax.dynamic_slice_in_dim(v_win_pad, t0, band, axis=1)
        key_pos = t0 - WINDOW + jnp.arange(band)
        win_ok = ((key_pos[None] <= t_pos[:, None]) & (key_pos[None] > t_pos[:, None] - WINDOW)
                  & (key_pos[None] >= 0))
        sw = jnp.einsum('btgrd,bsgd->bgrts', qc, kw).astype(jnp.float32) * scale
        p_win = jax.nn.softmax(jnp.where(win_ok, sw, NEG_INF), axis=-1)
        o_win = jnp.einsum('bgrts,bsgd->btgrd', p_win.astype(vw.dtype), vw)

        o = gc[..., 0:1] * o_cmp + gc[..., 1:2] * o_slc + gc[..., 2:3] * o_win
        return o.reshape(B, NSA_CHUNK, D_MODEL)

    o = lax.map(chunk, jnp.arange(S // NSA_CHUNK))
    o = o.transpose(1, 0, 2, 3).reshape(B, S, D_MODEL)
    return o @ w_out


def conformer_conv(u, w_in, b_in, dw, dw_b, ln_g, ln_b, w_out, b_out):
    a, g = jnp.split(u @ w_in + b_in, 2, axis=-1)
    h = a * jax.nn.sigmoid(g)
    h = lax.conv_general_dilated(h, dw, window_strides=(1,), padding=[(CONV_WIDTH - 1, 0)],
                                 dimension_numbers=('NWC', 'WIO', 'NWC'),
                                 feature_group_count=D_MODEL) + dw_b
    h = jax.nn.silu(layer_norm(h, ln_g, ln_b))
    return h @ w_out + b_out


def squared_relu_mlp(u, w1, w2):
    return jnp.square(jax.nn.relu(u @ w1)) @ w2


def setup_inputs(seed: int = 0) -> dict:
    key = jax.random.key(seed)
    keys = iter(jax.random.split(key, 40))

    def nrm(shape, scale):
        return jax.random.normal(next(keys), shape, jnp.float32) * scale

    def gain(shape):
        return 1.0 + nrm(shape, 0.05)

    D = D_MODEL
    return {
        'x': nrm((BATCH, SEQ, D), 1.0),
        'c': nrm((BATCH, D), 1.0),
        'positions': jnp.broadcast_to(jnp.arange(SEQ, dtype=jnp.int32), (BATCH, SEQ)),
        'ada_w': nrm((DEPTH, D, 6 * D), 0.5 * D ** -0.5),
        'ada_b': nrm((DEPTH, 6 * D), 0.01),
        'mix_pre_g': gain((DEPTH, D)),
        'mix_post_g': gain((DEPTH, D)),
        'ffn_pre_g': gain((DEPTH, D)),
        'ffn_post_g': gain((DEPTH, D)),
        'ffn_w1': nrm((DEPTH, D, D_FF), D ** -0.5),
        'ffn_w2': nrm((DEPTH, D_FF, D), D_FF ** -0.5),
        'sb_w_in': nrm((N_LAYERS_A, D, 3 * D), D ** -0.5),
        'sb_w_out': nrm((N_LAYERS_A, D, D), D ** -0.5),
        'nsa_w_in': nrm((N_LAYERS_B, D, NSA_IN), D ** -0.5),
        'nsa_w_out': nrm((N_LAYERS_B, D, D), D ** -0.5),
        'nsa_pe_k': nrm((N_LAYERS_B, CMP_LEN, HEAD_DIM), 0.1),
        'nsa_w1_k': nrm((N_LAYERS_B, CMP_LEN * HEAD_DIM, CMP_HIDDEN), (CMP_LEN * HEAD_DIM) ** -0.5),
        'nsa_w2_k': nrm((N_LAYERS_B, CMP_HIDDEN, HEAD_DIM), CMP_HIDDEN ** -0.5),
        'nsa_pe_v': nrm((N_LAYERS_B, CMP_LEN, HEAD_DIM), 0.1),
        'nsa_w1_v': nrm((N_LAYERS_B, CMP_LEN * HEAD_DIM, CMP_HIDDEN), (CMP_LEN * HEAD_DIM) ** -0.5),
        'nsa_w2_v': nrm((N_LAYERS_B, CMP_HIDDEN, HEAD_DIM), CMP_HIDDEN ** -0.5),
        'cv_w_in': nrm((N_LAYERS_C, D, 2 * D), D ** -0.5),
        'cv_b_in': nrm((N_LAYERS_C, 2 * D), 0.01),
        'cv_dw': nrm((N_LAYERS_C, CONV_WIDTH, 1, D), CONV_WIDTH ** -0.5),
        'cv_dw_b': nrm((N_LAYERS_C, D), 0.01),
        'cv_ln_g': gain((N_LAYERS_C, D)),
        'cv_ln_b': nrm((N_LAYERS_C, D), 0.01),
        'cv_w_out': nrm((N_LAYERS_C, D, D), D ** -0.5),
        'cv_b_out': nrm((N_LAYERS_C, D), 0.01),
    }


def reference(x, c, positions, ada_w, ada_b, mix_pre_g, mix_post_g, ffn_pre_g, ffn_post_g,
              ffn_w1, ffn_w2, sb_w_in, sb_w_out, nsa_w_in, nsa_w_out,
              nsa_pe_k, nsa_w1_k, nsa_w2_k, nsa_pe_v, nsa_w1_v, nsa_w2_v,
              cv_w_in, cv_b_in, cv_dw, cv_dw_b, cv_ln_g, cv_ln_b, cv_w_out, cv_b_out):
    cond = jax.nn.silu(c)
    h = x
    for i in range(DEPTH):
        mod = cond @ ada_w[i] + ada_b[i]
        sh1, sc1, g1, sh2, sc2, g2 = [m[:, None, :] for m in jnp.split(mod, 6, axis=-1)]

        u = rms_norm(h, mix_pre_g[i]) * (1.0 + sc1) + sh1
        kind, j = i % N_MIXERS, i // N_MIXERS
        if kind == 0:
            y = stick_breaking_attention(u, sb_w_in[j], sb_w_out[j])
        elif kind == 1:
            y = native_sparse_attention(u, positions, nsa_w_in[j], nsa_w_out[j],
                                        nsa_pe_k[j], nsa_w1_k[j], nsa_w2_k[j],
                                        nsa_pe_v[j], nsa_w1_v[j], nsa_w2_v[j])
        else:
            y = conformer_conv(u, cv_w_in[j], cv_b_in[j], cv_dw[j], cv_dw_b[j],
                               cv_ln_g[j], cv_ln_b[j], cv_w_out[j], cv_b_out[j])
        h = h + g1 * rms_norm(y, mix_post_g[i])

        u = rms_norm(h, ffn_pre_g[i]) * (1.0 + sc2) + sh2
        y = squared_relu_mlp(u, ffn_w1[i], ffn_w2[i])
        h = h + g2 * rms_norm(y, ffn_post_g[i])
    return h
```

```python
import functools

import numpy as np
import jax
import jax.numpy as jnp
from jax import lax
from jax.experimental import pallas as pl
from jax.experimental.pallas import tpu as pltpu

F32 = jnp.float32
BF16 = jnp.bfloat16

N_HEADS = 16
HEAD_DIM = 64
ROPE_THETA = 500000.0
ROPE_DIM = HEAD_DIM // 4
NORM_EPS = 1e-6
NSA_KV_GROUPS = 4
NSA_Q_PER_GROUP = N_HEADS // NSA_KV_GROUPS
CMP_LEN = 32
CMP_STRIDE = 16
SLC_LEN = 64
N_SELECT = 16
WINDOW = 512
FORCE_BONUS = 1e4
NEG_INF = -1e30
CONV_WIDTH = 31
ATTN_SCALE = HEAD_DIM ** -0.5

LANES = 128
VMEM_LIMIT_BYTES = 56 * 1024 * 1024

_NT = (((1,), (1,)), ((), ()))


def _cparams(n_axes):
    return pltpu.CompilerParams(dimension_semantics=("arbitrary",) * n_axes,
                                vmem_limit_bytes=VMEM_LIMIT_BYTES)


def _dot(a, b):
    return jnp.dot(a, b, preferred_element_type=F32)


def _dot_nt(a, b):
    return lax.dot_general(a, b, _NT, preferred_element_type=F32)


def _split_bf16(x):
    hi = x.astype(BF16)
    lo = (x - hi.astype(F32)).astype(BF16)
    return hi, lo


def _rms(x):
    return x * lax.rsqrt(jnp.mean(x * x, axis=-1, keepdims=True) + NORM_EPS)


def _norm_mod(h, g, sc, sh):
    return (_rms(h) * g) * (1.0 + sc) + sh


def _rope(x, cos_t, sin_t):
    rows, w = x.shape
    reps = w // LANES
    cos_w = jnp.concatenate([cos_t] * reps, axis=1) if reps > 1 else cos_t
    sin_w = jnp.concatenate([sin_t] * reps, axis=1) if reps > 1 else sin_t
    half = ROPE_DIM // 2
    lane = lax.broadcasted_iota(jnp.int32, (rows, w), 1)
    first_half = (lane & (HEAD_DIM - 1)) < half
    partner = jnp.where(first_half, pltpu.roll(x, w - half, 1), pltpu.roll(x, half, 1))
    return x * cos_w + partner * sin_w


def _ada_kernel(c_ref, w_ref, b_ref, o_ref):
    c = c_ref[...]
    cond = c * jax.nn.sigmoid(c)
    o_ref[0] = jnp.dot(cond, w_ref[0], preferred_element_type=F32,
                       precision=lax.Precision.HIGHEST) + b_ref[0]


def _ada_mod(c, ada_w, ada_b):
    depth, d, n = ada_w.shape
    b = c.shape[0]
    tn = 1024
    return pl.pallas_call(
        _ada_kernel,
        out_shape=jax.ShapeDtypeStruct((depth, b, n), F32),
        grid=(depth, n // tn),
        in_specs=[pl.BlockSpec((b, d), lambda i, j: (0, 0)),
                  pl.BlockSpec((1, d, tn), lambda i, j: (i, 0, j)),
                  pl.BlockSpec((1, 1, tn), lambda i, j: (i, 0, j))],
        out_specs=pl.BlockSpec((1, b, tn), lambda i, j: (i, 0, j)),
        compiler_params=_cparams(2),
        name="ada_mod",
    )(c, ada_w, ada_b.reshape(depth, 1, n))


def _proj_in_specs(tm, d, n_w):
    return [pl.BlockSpec((1, tm, d), lambda b, i: (b, i, 0)),
            pl.BlockSpec((1, d), lambda b, i: (0, 0)),
            pl.BlockSpec((1, 1, d), lambda b, i: (b, 0, 0)),
            pl.BlockSpec((1, 1, d), lambda b, i: (b, 0, 0)),
            pl.BlockSpec((d, n_w), lambda b, i: (0, 0))]


def _sb_proj_kernel(h_ref, g_ref, sc_ref, sh_ref, w_ref, o_ref, *, tn):
    d = h_ref.shape[2]
    u = _norm_mod(h_ref[0], g_ref[...], sc_ref[0], sh_ref[0]).astype(BF16)
    for j in range(w_ref.shape[1] // tn):
        acc = _dot(u, w_ref[:, j * tn:(j + 1) * tn])
        if j * tn < d:
            acc = acc * ATTN_SCALE
        o_ref[0, :, j * tn:(j + 1) * tn] = acc.astype(BF16)


def _sb_proj(h, g, sc, sh, w, tm=512, tn=512):
    bsz, s, d = h.shape
    n = w.shape[1]
    return pl.pallas_call(
        functools.partial(_sb_proj_kernel, tn=tn),
        out_shape=jax.ShapeDtypeStruct((bsz, s, n), BF16),
        grid=(bsz, s // tm),
        in_specs=_proj_in_specs(tm, d, n),
        out_specs=pl.BlockSpec((1, tm, n), lambda b, i: (b, i, 0)),
        compiler_params=_cparams(2),
        name="sb_proj",
    )(h, g, sc, sh, w)


def _nsa_proj_kernel(h_ref, g_ref, sc_ref, sh_ref, w_ref, cos_ref, sin_ref,
                     q_ref, kv_ref, gate_ref, *, tn):
    d = h_ref.shape[2]
    kv_w = NSA_KV_GROUPS * HEAD_DIM
    u = _norm_mod(h_ref[0], g_ref[...], sc_ref[0], sh_ref[0]).astype(BF16)
    cos_t = cos_ref[0]
    sin_t = sin_ref[0]
    n_q = d // tn
    n_kv = 6 * kv_w // tn
    for j in range(w_ref.shape[1] // tn):
        acc = _dot(u, w_ref[:, j * tn:(j + 1) * tn])
        if j < n_q:
            q = _rope(acc, cos_t, sin_t) * ATTN_SCALE
            q_ref[0, :, j * tn:(j + 1) * tn] = q.astype(BF16)
        elif j < n_q + n_kv:
            for part in range(tn // kv_w):
                i = (j - n_q) * (tn // kv_w) + part
                x = acc[:, part * kv_w:(part + 1) * kv_w]
                if i in (2, 4):
                    x = _rope(x, cos_t, sin_t)
                for grp in range(NSA_KV_GROUPS):
                    kv_ref[0, i, grp] = x[:, grp * HEAD_DIM:(grp + 1) * HEAD_DIM].astype(BF16)
        else:
            c0 = (j - n_q - n_kv) * tn
            gate_ref[0, :, c0:c0 + tn] = jax.nn.sigmoid(acc)


def _nsa_proj(h, g, sc, sh, w, cos_t, sin_t, tm=512, tn=256):
    bsz, s, d = h.shape
    n = w.shape[1]
    n_gate = NSA_KV_GROUPS * LANES
    return pl.pallas_call(
        functools.partial(_nsa_proj_kernel, tn=tn),
        out_shape=(jax.ShapeDtypeStruct((bsz, s, d), BF16),
                   jax.ShapeDtypeStruct((bsz, 6, NSA_KV_GROUPS, s, HEAD_DIM), BF16),
                   jax.ShapeDtypeStruct((bsz, s, n_gate), F32)),
        grid=(bsz, s // tm),
        in_specs=_proj_in_specs(tm, d, n) + [
            pl.BlockSpec((1, tm, LANES), lambda b, i: (b, i, 0)),
            pl.BlockSpec((1, tm, LANES), lambda b, i: (b, i, 0))],
        out_specs=(pl.BlockSpec((1, tm, d), lambda b, i: (b, i, 0)),
                   pl.BlockSpec((1, 6, NSA_KV_GROUPS, tm, HEAD_DIM), lambda b, i: (b, 0, 0, i, 0)),
                   pl.BlockSpec((1, tm, n_gate), lambda b, i: (b, i, 0))),
        compiler_params=_cparams(2),
        name="nsa_proj",
    )(h, g, sc, sh, w, cos_t, sin_t)


def _cv_proj_kernel(h_ref, g_ref, sc_ref, sh_ref, w_ref, b_ref, o_ref, *, tn):
    d = h_ref.shape[2]
    u = _norm_mod(h_ref[0], g_ref[...], sc_ref[0], sh_ref[0]).astype(BF16)
    for j in range(d // tn):
        a = _dot(u, w_ref[:, j * tn:(j + 1) * tn]) + b_ref[:, j * tn:(j + 1) * tn]
        gt = _dot(u, w_ref[:, d + j * tn:d + (j + 1) * tn]) + b_ref[:, d + j * tn:d + (j + 1) * tn]
        o_ref[0, :, j * tn:(j + 1) * tn] = a * jax.nn.sigmoid(gt)


def _cv_proj(h, g, sc, sh, w, bias, tm=512, tn=256):
    bsz, s, d = h.shape
    n = w.shape[1]
    return pl.pallas_call(
        functools.partial(_cv_proj_kernel, tn=tn),
        out_shape=jax.ShapeDtypeStruct((bsz, s, d), F32),
        grid=(bsz, s // tm),
        in_specs=_proj_in_specs(tm, d, n) + [pl.BlockSpec((1, n), lambda b, i: (0, 0))],
        out_specs=pl.BlockSpec((1, tm, d), lambda b, i: (b, i, 0)),
        compiler_params=_cparams(2),
        name="cv_proj",
    )(h, g, sc, sh, w, bias)


def _out_proj_kernel(a_ref, w_ref, b_ref, h_ref, gate_ref, pg_ref, o_ref, *, tn):
    a = a_ref[0]
    d = w_ref.shape[1]
    for j in range(d // tn):
        o_ref[0, :, j * tn:(j + 1) * tn] = (_dot(a, w_ref[:, j * tn:(j + 1) * tn])
                                            + b_ref[:, j * tn:(j + 1) * tn])
    y = o_ref[0]
    o_ref[0] = h_ref[0] + gate_ref[0] * (_rms(y) * pg_ref[...])


def _out_proj(a, w, bias, h, gate, post_g, tm=512, tn=256):
    bsz, s, k = a.shape
    d = w.shape[1]
    return pl.pallas_call(
        functools.partial(_out_proj_kernel, tn=tn),
        out_shape=jax.ShapeDtypeStruct((bsz, s, d), F32),
        grid=(bsz, s // tm),
        in_specs=[pl.BlockSpec((1, tm, k), lambda b, i: (b, i, 0)),
                  pl.BlockSpec((k, d), lambda b, i: (0, 0)),
                  pl.BlockSpec((1, d), lambda b, i: (0, 0)),
                  pl.BlockSpec((1, tm, d), lambda b, i: (b, i, 0)),
                  pl.BlockSpec((1, 1, d), lambda b, i: (b, 0, 0)),
                  pl.BlockSpec((1, d), lambda b, i: (0, 0))],
        out_specs=pl.BlockSpec((1, tm, d), lambda b, i: (b, i, 0)),
        compiler_params=_cparams(2),
        name="out_proj",
    )(a, w, bias, h, gate, post_g)


def _mlp_kernel(h_ref, g_ref, sc_ref, sh_ref, w1_ref, w2_ref, gate_ref, pg_ref, o_ref, hid_ref,
                *, tf, tn):
    h = h_ref[0]
    u = _norm_mod(h, g_ref[...], sc_ref[0], sh_ref[0]).astype(BF16)
    d_ff = w1_ref.shape[1]
    d = w2_ref.shape[1]
    for c in range(d_ff // tf):
        a = jnp.maximum(_dot(u, w1_ref[:, c * tf:(c + 1) * tf]), 0.0)
        hid_ref[:, c * tf:(c + 1) * tf] = (a * a).astype(BF16)
    for j in range(d // tn):
        o_ref[0, :, j * tn:(j + 1) * tn] = _dot(hid_ref[...], w2_ref[:, j * tn:(j + 1) * tn])
    y = o_ref[0]
    o_ref[0] = h + gate_ref[0] * (_rms(y) * pg_ref[...])


def _mlp(h, g, sc, sh, w1, w2, gate, post_g, tm=512, tf=512, tn=256):
    bsz, s, d = h.shape
    d_ff = w1.shape[1]
    return pl.pallas_call(
        functools.partial(_mlp_kernel, tf=tf, tn=tn),
        out_shape=jax.ShapeDtypeStruct((bsz, s, d), F32),
        grid=(bsz, s // tm),
        in_specs=[pl.BlockSpec((1, tm, d), lambda b, i: (b, i, 0)),
                  pl.BlockSpec((1, d), lambda b, i: (0, 0)),
                  pl.BlockSpec((1, 1, d), lambda b, i: (b, 0, 0)),
                  pl.BlockSpec((1, 1, d), lambda b, i: (b, 0, 0)),
                  pl.BlockSpec((d, d_ff), lambda b, i: (0, 0), pipeline_mode=pl.Buffered(1)),
                  pl.BlockSpec((d_ff, d), lambda b, i: (0, 0), pipeline_mode=pl.Buffered(1)),
                  pl.BlockSpec((1, 1, d), lambda b, i: (b, 0, 0)),
                  pl.BlockSpec((1, d), lambda b, i: (0, 0))],
        out_specs=pl.BlockSpec((1, tm, d), lambda b, i: (b, i, 0)),
        scratch_shapes=[pltpu.VMEM((tm, d_ff), BF16)],
        compiler_params=_cparams(2),
        name="mlp",
    )(h, g, sc, sh, w1, w2, gate, post_g)


def _sb_tile(qh, k, v, tri, carry, acc, strict):
    z = _dot_nt(qh, k)
    lk = -(jnp.maximum(z, 0.0) + jnp.log(1.0 + jnp.exp(-jnp.abs(z))))
    if strict is not None:
        lk = jnp.where(strict, lk, 0.0)
    hi, lo = _split_bf16(lk)
    tail = _dot(hi, tri) + _dot(lo, tri)
    a = jnp.exp(z + lk + tail + carry)
    if strict is not None:
        a = jnp.where(strict, a, 0.0)
    acc = acc + _dot(a.astype(BF16), v)
    carry = carry + jnp.sum(lk, axis=-1, keepdims=True)
    return carry, acc


def _sb_attn_kernel(q_ref, k_ref, v_ref, tri_ref, o_ref, *, t):
    qi = pl.program_id(2)
    q = q_ref[0]
    tri = tri_ref[...]
    lane = lax.broadcasted_iota(jnp.int32, (t, LANES), 1)
    row = lax.broadcasted_iota(jnp.int32, (t, t), 0)
    col = lax.broadcasted_iota(jnp.int32, (t, t), 1)
    strict = col < row
    zero_q = jnp.zeros_like(q)
    accs = []
    for hh in range(2):
        in_head = (lane < HEAD_DIM) if hh == 0 else (lane >= HEAD_DIM)
        qh = jnp.where(in_head, q, zero_q)

        def load(kj):
            start = pl.multiple_of(kj * t, t)
            return k_ref[0, pl.ds(start, t), :], v_ref[0, pl.ds(start, t), :]

        k, v = load(qi)
        carry, acc = _sb_tile(qh, k, v, tri, jnp.zeros((t, 1), F32), jnp.zeros((t, LANES), F32), strict)

        def body(i, state, qh=qh):
            k, v = load(qi - 1 - i)
            return _sb_tile(qh, k, v, tri, state[0], state[1], None)

        carry, acc = lax.fori_loop(0, qi, body, (carry, acc))
        accs.append(acc)
    o_ref[0] = jnp.where(lane < HEAD_DIM, accs[0], accs[1]).astype(BF16)


def _sb_attention(qkv, t=256):
    bsz, s, n3 = qkv.shape
    d = n3 // 3
    n_pairs = d // LANES
    t = min(t, s)
    tri = jnp.asarray(np.tril(np.ones((t, t), np.float32), -1), BF16)
    return pl.pallas_call(
        functools.partial(_sb_attn_kernel, t=t),
        out_shape=jax.ShapeDtypeStruct((bsz, s, d), BF16),
        grid=(bsz, n_pairs, s // t),
        in_specs=[pl.BlockSpec((1, t, LANES), lambda b, p, i: (b, i, p)),
                  pl.BlockSpec((1, s, LANES), lambda b, p, i: (b, 0, n_pairs + p)),
                  pl.BlockSpec((1, s, LANES), lambda b, p, i: (b, 0, 2 * n_pairs + p)),
                  pl.BlockSpec((t, t), lambda b, p, i: (0, 0))],
        out_specs=pl.BlockSpec((1, t, LANES), lambda b, p, i: (b, i, p)),
        compiler_params=_cparams(3),
        name="sb_attention",
    )(qkv, qkv, qkv, tri)


def _compress_kernel(x_ref, pe_ref, w1_ref, w2_ref, cos_ref, sin_ref, o_ref, *, rope):
    x = x_ref[0, 0, 0]
    w1 = w1_ref[...]
    hid = w1.shape[1] // 2
    n_seg = x.shape[0]
    pre = _dot(x, w1)
    pe_term = _dot(pe_ref[...], w1)
    bias = pe_term[0:1, :hid] + pe_term[8:9, hid:]
    nxt = pltpu.roll(pre[:, hid:], n_seg - 1, 0)
    mid = jax.nn.gelu(pre[:, :hid] + nxt + bias)
    out = _dot(mid.astype(BF16), w2_ref[...])
    if rope:
        out = _rope(out, cos_ref[0], sin_ref[0])
    o_ref[0, 0] = out[:, :HEAD_DIM].astype(BF16)


def _compress(kv_seg, which, pe, w1, w2, cos_c, sin_c, rope):
    bsz, _, grp, n_seg, seg_w = kv_seg.shape
    hid = w1.shape[1]
    half = w1.shape[0] // 2
    w1cat = jnp.concatenate([w1[:half], w1[half:]], axis=1).astype(BF16)
    w2p = jnp.pad(w2, ((0, 0), (0, LANES - HEAD_DIM))).astype(BF16)
    pe_flat = pe.reshape(2, half)
    pe_rows = jnp.zeros((16, half), F32).at[0].set(pe_flat[0]).at[8].set(pe_flat[1]).astype(BF16)
    return pl.pallas_call(
        functools.partial(_compress_kernel, rope=rope),
        out_shape=jax.ShapeDtypeStruct((bsz, grp, n_seg, HEAD_DIM), BF16),
        grid=(bsz, grp),
        in_specs=[pl.BlockSpec((1, 1, 1, n_seg, seg_w), lambda b, g: (b, which, g, 0, 0)),
                  pl.BlockSpec((16, half), lambda b, g: (0, 0)),
                  pl.BlockSpec((half, 2 * hid), lambda b, g: (0, 0)),
                  pl.BlockSpec((hid, LANES), lambda b, g: (0, 0)),
                  pl.BlockSpec((1, n_seg, LANES), lambda b, g: (b, 0, 0)),
                  pl.BlockSpec((1, n_seg, LANES), lambda b, g: (b, 0, 0))],
        out_specs=pl.BlockSpec((1, 1, n_seg, HEAD_DIM), lambda b, g: (b, g, 0, 0)),
        compiler_params=_cparams(2),
        name="nsa_compress",
    )(kv_seg, pe_rows, w1cat, w2p, cos_c, sin_c)


def _online_softmax_step(s, v, m, l, acc):
    m_new = jnp.maximum(m, jnp.max(s, axis=-1, keepdims=True))
    alpha = jnp.exp(m - m_new)
    p = jnp.exp(s - m_new)
    l = alpha * l + jnp.sum(p, axis=-1, keepdims=True)
    acc = alpha * acc + _dot(p.astype(BF16), v)
    return m_new, l, acc


def _nsa_attn_kernel(q_ref, kc_ref, vc_ref, ks_ref, vs_ref, kw_ref, vw_ref, gate_ref,
                     ovt_ref, eye_ref, exp_ref, o_ref, *, tq, tk, n_cmp, n_sel):
    qi = pl.program_id(2)
    r_heads = NSA_Q_PER_GROUP
    rows = r_heads * tq
    t0 = qi * tq
    n_cp = kc_ref.shape[2]
    n_slc = ovt_ref.shape[0]

    qf = q_ref[0].astype(F32)
    q4 = jnp.concatenate([qf[:, r * HEAD_DIM:(r + 1) * HEAD_DIM] for r in range(r_heads)],
                         axis=0).astype(BF16)

    kc = kc_ref[0, 0]
    vc = vc_ref[0, 0]
    sc = _dot_nt(q4, kc).reshape(r_heads, tq, n_cp)
    t_c = t0 + lax.broadcasted_iota(jnp.int32, (tq, n_cp), 0)
    n_c = lax.broadcasted_iota(jnp.int32, (tq, n_cp), 1)
    cmp_ok = (n_c * CMP_STRIDE + (CMP_LEN - 1) <= t_c) & (n_c < n_cmp)
    any_ok = (t_c[:, 0:1] >= CMP_LEN - 1).astype(F32)
    sc = jnp.where(cmp_ok[None], sc, NEG_INF)
    e = jnp.exp(sc - jnp.max(sc, axis=-1, keepdims=True))
    p_cmp = e / jnp.sum(e, axis=-1, keepdims=True) * any_ok[None]
    o_cmp = _dot(p_cmp.reshape(rows, n_cp).astype(BF16), vc)

    p_grp = jnp.sum(p_cmp, axis=0)
    p_hi, p_lo = _split_bf16(p_grp)
    ovt = ovt_ref[...]
    imp_t = _dot_nt(ovt, p_hi) + _dot_nt(ovt, p_lo)
    blk = lax.broadcasted_iota(jnp.int32, (n_slc, tq), 0)
    tok = t0 + lax.broadcasted_iota(jnp.int32, (n_slc, tq), 1)
    cur = lax.shift_right_logical(tok, SLC_LEN.bit_length() - 1)
    forced = (blk == 0) | (blk == cur) | (blk == cur - 1)
    score = jnp.where(blk * SLC_LEN <= tok, jnp.where(forced, FORCE_BONUS, imp_t), NEG_INF)
    rank = jnp.zeros((n_slc, tq), jnp.int32)
    for i in range(n_slc):
        s_i = score[i:i + 1, :]
        ahead = (s_i > score) | ((s_i == score) & (blk > i))
        rank = rank + jnp.where(ahead, 1, 0)
    neg_t = jnp.where(rank < n_sel, 0.0, NEG_INF).astype(BF16)
    neg = _dot_nt(eye_ref[...], neg_t).astype(BF16)

    t_k = t0 + lax.broadcasted_iota(jnp.int32, (tq, tk), 0)
    c_k = lax.broadcasted_iota(jnp.int32, (tq, tk), 1)

    def slc_step(kj, state, causal):
        start = pl.multiple_of(kj * tk, tk)
        k = ks_ref[0, 0, pl.ds(start, tk), :]
        v = vs_ref[0, 0, pl.ds(start, tk), :]
        s = _dot_nt(q4, k).reshape(r_heads, tq, tk)
        madd = _dot(neg, exp_ref[kj])
        if causal:
            madd = jnp.where(c_k + kj * tk <= t_k, madd, NEG_INF)
        s = (s + madd[None]).reshape(rows, tk)
        return _online_softmax_step(s, v, *state)

    init = (jnp.full((rows, 1), NEG_INF, F32), jnp.zeros((rows, 1), F32),
            jnp.zeros((rows, HEAD_DIM), F32))
    last = (t0 + tq - 1) // tk
    state = lax.fori_loop(0, last, lambda kj, st: slc_step(kj, st, False), init)
    _, l_s, acc_s = slc_step(last, state, True)
    o_slc = acc_s / l_s

    def win_step(kj, state):
        start = pl.multiple_of(kj * tk, tk)
        k = kw_ref[0, 0, pl.ds(start, tk), :]
        v = vw_ref[0, 0, pl.ds(start, tk), :]
        s = _dot_nt(q4, k).reshape(r_heads, tq, tk)
        key = c_k + kj * tk
        ok = (key <= t_k) & (key > t_k - WINDOW)
        s = jnp.where(ok[None], s, NEG_INF).reshape(rows, tk)
        return _online_softmax_step(s, v, *state)

    first = jnp.maximum(t0 - (WINDOW - 1), 0) // tk
    _, l_w, acc_w = lax.fori_loop(first, last + 1, win_step, init)
    o_win = acc_w / l_w

    gates = gate_ref[0]
    pieces = []
    for r in range(r_heads):
        sl = slice(r * tq, (r + 1) * tq)
        pieces.append(gates[:, r:r + 1] * o_cmp[sl]
                      + gates[:, r_heads + r:r_heads + r + 1] * o_slc[sl]
                      + gates[:, 2 * r_heads + r:2 * r_heads + r + 1] * o_win[sl])
    o_ref[0] = jnp.concatenate(pieces, axis=1).astype(BF16)


def _nsa_attention(q, kv6, k_cmp, v_cmp, gates, tq=128, tk=256):
    bsz, s, d = q.shape
    grp = NSA_KV_GROUPS
    gw = NSA_Q_PER_GROUP * HEAD_DIM
    n_cp = k_cmp.shape[2]
    n_cmp = (s - CMP_LEN) // CMP_STRIDE + 1
    n_slc = s // SLC_LEN
    n_sel = min(N_SELECT, n_slc)
    tq = min(tq, s)
    tk = min(tk, s)
    c0 = np.arange(n_cp)[None, :] * CMP_STRIDE
    s0 = np.arange(n_slc)[:, None] * SLC_LEN
    ov = np.maximum(np.minimum(c0 + CMP_LEN, s0 + SLC_LEN) - np.maximum(c0, s0), 0) / CMP_LEN
    ov = ov * (np.arange(n_cp)[None, :] < n_cmp)
    ovt = jnp.asarray(ov, BF16)
    eye = jnp.asarray(np.eye(tq, dtype=np.float32), BF16)
    n_kt = s // tk
    expand = (np.arange(n_slc)[None, :, None]
              == (np.arange(n_kt)[:, None, None] * (tk // SLC_LEN) + np.arange(tk)[None, None, :] // SLC_LEN))
    expand = jnp.asarray(expand.astype(np.float32), BF16)

    def kv_spec(i):
        return pl.BlockSpec((1, 1, 1, s, HEAD_DIM), lambda b, g, t, i=i: (b, i, g, 0, 0))

    def kernel(q_ref, kc_ref, vc_ref, ks_ref, vs_ref, kw_ref, vw_ref, *rest):
        return _nsa_attn_kernel(q_ref, kc_ref, vc_ref, ks_ref.at[0], vs_ref.at[0], kw_ref.at[0],
                                vw_ref.at[0], *rest, tq=tq, tk=tk, n_cmp=n_cmp, n_sel=n_sel)

    return pl.pallas_call(
        kernel,
        out_shape=jax.ShapeDtypeStruct((bsz, s, d), BF16),
        grid=(bsz, grp, s // tq),
        in_specs=[pl.BlockSpec((1, tq, gw), lambda b, g, t: (b, t, g)),
                  pl.BlockSpec((1, 1, n_cp, HEAD_DIM), lambda b, g, t: (b, g, 0, 0)),
                  pl.BlockSpec((1, 1, n_cp, HEAD_DIM), lambda b, g, t: (b, g, 0, 0)),
                  kv_spec(2), kv_spec(3), kv_spec(4), kv_spec(5),
                  pl.BlockSpec((1, tq, LANES), lambda b, g, t: (b, t, g)),
                  pl.BlockSpec((n_slc, n_cp), lambda b, g, t: (0, 0)),
                  pl.BlockSpec((tq, tq), lambda b, g, t: (0, 0)),
                  pl.BlockSpec((n_kt, n_slc, tk), lambda b, g, t: (0, 0, 0))],
        out_specs=pl.BlockSpec((1, tq, gw), lambda b, g, t: (b, t, g)),
        compiler_params=_cparams(3),
        name="nsa_attention",
    )(q, k_cmp, v_cmp, kv6, kv6, kv6, kv6, gates, ovt, eye, expand)


_HALO = 32


def _dwconv_kernel(x_ref, prev_ref, dw_ref, dwb_ref, lng_ref, lnb_ref, o_ref, ext_ref, *, tm):
    i = pl.program_id(1)
    prev = prev_ref[0]
    ext_ref[0:_HALO, :] = jnp.where(i > 0, prev, jnp.zeros_like(prev))
    ext_ref[_HALO:_HALO + tm, :] = x_ref[0]
    off = _HALO - (CONV_WIDTH - 1)
    acc = jnp.zeros(x_ref.shape[1:], F32) + dwb_ref[...]
    for k in range(CONV_WIDTH):
        acc = acc + ext_ref[off + k:off + k + tm, :] * dw_ref[k:k + 1, :]
    mu = jnp.mean(acc, axis=-1, keepdims=True)
    cen = acc - mu
    var = jnp.mean(cen * cen, axis=-1, keepdims=True)
    y = cen * lax.rsqrt(var + NORM_EPS) * lng_ref[...] + lnb_ref[...]
    o_ref[0] = (y * jax.nn.sigmoid(y)).astype(BF16)


def _dwconv_ln_swish(x, dw, dw_b, ln_g, ln_b, tm=256):
    bsz, s, d = x.shape
    tm = min(tm, s)
    per = tm // _HALO
    return pl.pallas_call(
        functools.partial(_dwconv_kernel, tm=tm),
        out_shape=jax.ShapeDtypeStruct((bsz, s, d), BF16),
        grid=(bsz, s // tm),
        in_specs=[pl.BlockSpec((1, tm, d), lambda b, i: (b, i, 0)),
                  pl.BlockSpec((1, _HALO, d), lambda b, i: (b, jnp.maximum(i * per - 1, 0), 0)),
                  pl.BlockSpec((_HALO, d), lambda b, i: (0, 0)),
                  pl.BlockSpec((1, d), lambda b, i: (0, 0)),
                  pl.BlockSpec((1, d), lambda b, i: (0, 0)),
                  pl.BlockSpec((1, d), lambda b, i: (0, 0))],
        out_specs=pl.BlockSpec((1, tm, d), lambda b, i: (b, i, 0)),
        scratch_shapes=[pltpu.VMEM((_HALO + tm, d), F32)],
        compiler_params=_cparams(2),
        name="dwconv_ln_swish",
    )(x, x, dw, dw_b, ln_g, ln_b)


def _rope_tables(positions):
    half = ROPE_DIM // 2
    inv_freq = ROPE_THETA ** (-jnp.arange(half, dtype=F32) / half)
    ang = positions.astype(F32)[..., None] * inv_freq
    cos, sin = jnp.cos(ang), jnp.sin(ang)
    ones = jnp.ones(ang.shape[:-1] + (HEAD_DIM - ROPE_DIM,), F32)
    cos_h = jnp.concatenate([cos, cos, ones], axis=-1)
    sin_h = jnp.concatenate([-sin, sin, 0.0 * ones], axis=-1)
    reps = LANES // HEAD_DIM
    return jnp.concatenate([cos_h] * reps, axis=-1), jnp.concatenate([sin_h] * reps, axis=-1)


def _nsa_weight(w_in, d):
    kv_end = d + 6 * NSA_KV_GROUPS * HEAD_DIM
    wg = w_in[:, kv_end:].reshape(d, NSA_KV_GROUPS, NSA_Q_PER_GROUP, 3)
    wg = wg.transpose(0, 1, 3, 2).reshape(d, NSA_KV_GROUPS, 3 * NSA_Q_PER_GROUP)
    wg = jnp.pad(wg, ((0, 0), (0, 0), (0, LANES - 3 * NSA_Q_PER_GROUP)))
    return jnp.concatenate([w_in[:, :kv_end], wg.reshape(d, NSA_KV_GROUPS * LANES)], axis=1).astype(BF16)


def kernel(x, c, positions, ada_w, ada_b, mix_pre_g, mix_post_g, ffn_pre_g, ffn_post_g, ffn_w1, ffn_w2, sb_w_in, sb_w_out, nsa_w_in, nsa_w_out, nsa_pe_k, nsa_w1_k, nsa_w2_k, nsa_pe_v, nsa_w1_v, nsa_w2_v, cv_w_in, cv_b_in, cv_dw, cv_dw_b, cv_ln_g, cv_ln_b, cv_w_out, cv_b_out):
    bsz, s, d = x.shape
    depth = ada_w.shape[0]
    n_mixers = 3
    mod = _ada_mod(c, ada_w, ada_b).reshape(depth, bsz, 6, 1, d)
    zero_bias = jnp.zeros((1, d), F32)
    h = x
    for i in range(depth):
        sh1, sc1, g1, sh2, sc2, g2 = [mod[i, :, m] for m in range(6)]
        pre_g = mix_pre_g[i].reshape(1, d)
        post_g = mix_post_g[i].reshape(1, d)
        kind, j = i % n_mixers, i // n_mixers
        if kind == 0:
            qkv = _sb_proj(h, pre_g, sc1, sh1, sb_w_in[j].astype(BF16))
            o = _sb_attention(qkv)
            h = _out_proj(o, sb_w_out[j].astype(BF16), zero_bias, h, g1, post_g)
        elif kind == 1:
            cos_t, sin_t = _rope_tables(positions)
            q, kv6, gates = _nsa_proj(h, pre_g, sc1, sh1, _nsa_weight(nsa_w_in[j], d), cos_t, sin_t)
            n_seg = s // CMP_STRIDE
            kv_seg = kv6.reshape(bsz, 6, NSA_KV_GROUPS, n_seg, CMP_STRIDE * HEAD_DIM)
            end = jnp.minimum(jnp.arange(n_seg) * CMP_STRIDE + CMP_LEN - 1, s - 1)
            cos_c, sin_c = cos_t[:, end], sin_t[:, end]
            k_cmp = _compress(kv_seg, 0, nsa_pe_k[j], nsa_w1_k[j], nsa_w2_k[j], cos_c, sin_c, True)
            v_cmp = _compress(kv_seg, 1, nsa_pe_v[j], nsa_w1_v[j], nsa_w2_v[j], cos_c, sin_c, False)
            o = _nsa_attention(q, kv6, k_cmp, v_cmp, gates)
            h = _out_proj(o, nsa_w_out[j].astype(BF16), zero_bias, h, g1, post_g)
        else:
            a = _cv_proj(h, pre_g, sc1, sh1, cv_w_in[j].astype(BF16), cv_b_in[j].reshape(1, 2 * d))
            dw = jnp.pad(cv_dw[j].reshape(CONV_WIDTH, d), ((0, _HALO - CONV_WIDTH), (0, 0)))
            a = _dwconv_ln_swish(a, dw, cv_dw_b[j].reshape(1, d), cv_ln_g[j].reshape(1, d),
                                 cv_ln_b[j].reshape(1, d))
            h = _out_proj(a, cv_w_out[j].astype(BF16), cv_b_out[j].reshape(1, d), h, g1, post_g)
        h = _mlp(h, ffn_pre_g[i].reshape(1, d), sc2, sh2, ffn_w1[i].astype(BF16),
                 ffn_w2[i].astype(BF16), g2, ffn_post_g[i].reshape(1, d))
    return h
```

```python
import functools

import numpy as np
import jax
import jax.numpy as jnp
from jax import lax
from jax.experimental import pallas as pl
from jax.experimental.pallas import tpu as pltpu

F32 = jnp.float32
BF16 = jnp.bfloat16

N_HEADS = 16
HEAD_DIM = 64
ROPE_THETA = 500000.0
ROPE_DIM = HEAD_DIM // 4
NORM_EPS = 1e-6
NSA_KV_GROUPS = 4
NSA_Q_PER_GROUP = N_HEADS // NSA_KV_GROUPS
CMP_LEN = 32
CMP_STRIDE = 16
SLC_LEN = 64
N_SELECT = 16
WINDOW = 512
FORCE_BONUS = 1e4
NEG_INF = -1e30
CONV_WIDTH = 31
ATTN_SCALE = HEAD_DIM ** -0.5

LANES = 128
VMEM_LIMIT_BYTES = 56 * 1024 * 1024

_NT = (((1,), (1,)), ((), ()))


def _cparams(n_axes):
    return pltpu.CompilerParams(dimension_semantics=("arbitrary",) * n_axes,
                                vmem_limit_bytes=VMEM_LIMIT_BYTES)


def _dot(a, b):
    return jnp.dot(a, b, preferred_element_type=F32)


def _dot_nt(a, b):
    return lax.dot_general(a, b, _NT, preferred_element_type=F32)


def _split_bf16(x):
    hi = x.astype(BF16)
    lo = (x - hi.astype(F32)).astype(BF16)
    return hi, lo


def _rms(x):
    return x * lax.rsqrt(jnp.mean(x * x, axis=-1, keepdims=True) + NORM_EPS)


def _norm_mod(h, g, sc, sh):
    return (_rms(h) * g) * (1.0 + sc) + sh


def _rope(x, cos_t, sin_t):
    rows, w = x.shape
    reps = w // LANES
    cos_w = jnp.concatenate([cos_t] * reps, axis=1) if reps > 1 else cos_t
    sin_w = jnp.concatenate([sin_t] * reps, axis=1) if reps > 1 else sin_t
    half = ROPE_DIM // 2
    lane = lax.broadcasted_iota(jnp.int32, (rows, w), 1)
    first_half = (lane & (HEAD_DIM - 1)) < half
    partner = jnp.where(first_half, pltpu.roll(x, w - half, 1), pltpu.roll(x, half, 1))
    return x * cos_w + partner * sin_w


def _ada_kernel(c_ref, w_ref, b_ref, o_ref):
    c = c_ref[...]
    cond = c * jax.nn.sigmoid(c)
    o_ref[0] = jnp.dot(cond, w_ref[0], preferred_element_type=F32,
                       precision=lax.Precision.HIGHEST) + b_ref[0]


def _ada_mod(c, ada_w, ada_b):
    depth, d, n = ada_w.shape
    b = c.shape[0]
    tn = 1024
    return pl.pallas_call(
        _ada_kernel,
        out_shape=jax.ShapeDtypeStruct((depth, b, n), F32),
        grid=(depth, n // tn),
        in_specs=[pl.BlockSpec((b, d), lambda i, j: (0, 0)),
                  pl.BlockSpec((1, d, tn), lambda i, j: (i, 0, j)),
                  pl.BlockSpec((1, 1, tn), lambda i, j: (i, 0, j))],
        out_specs=pl.BlockSpec((1, b, tn), lambda i, j: (i, 0, j)),
        compiler_params=_cparams(2),
        name="ada_mod",
    )(c, ada_w, ada_b.reshape(depth, 1, n))


def _proj_in_specs(tm, d, n_w):
    return [pl.BlockSpec((1, tm, d), lambda b, i: (b, i, 0)),
            pl.BlockSpec((1, d), lambda b, i: (0, 0)),
            pl.BlockSpec((1, 1, d), lambda b, i: (b, 0, 0)),
            pl.BlockSpec((1, 1, d), lambda b, i: (b, 0, 0)),
            pl.BlockSpec((d, n_w), lambda b, i: (0, 0))]


def _sb_proj_kernel(h_ref, g_ref, sc_ref, sh_ref, w_ref, o_ref, *, tn):
    d = h_ref.shape[2]
    u = _norm_mod(h_ref[0], g_ref[...], sc_ref[0], sh_ref[0]).astype(BF16)
    for j in range(w_ref.shape[1] // tn):
        acc = _dot(u, w_ref[:, j * tn:(j + 1) * tn])
        if j * tn < d:
            acc = acc * (-ATTN_SCALE)
        o_ref[0, :, j * tn:(j + 1) * tn] = acc.astype(BF16)


def _sb_proj(h, g, sc, sh, w, tm=512, tn=512):
    bsz, s, d = h.shape
    n = w.shape[1]
    return pl.pallas_call(
        functools.partial(_sb_proj_kernel, tn=tn),
        out_shape=jax.ShapeDtypeStruct((bsz, s, n), BF16),
        grid=(bsz, s // tm),
        in_specs=_proj_in_specs(tm, d, n),
        out_specs=pl.BlockSpec((1, tm, n), lambda b, i: (b, i, 0)),
        compiler_params=_cparams(2),
        name="sb_proj",
    )(h, g, sc, sh, w)


def _nsa_proj_kernel(h_ref, g_ref, sc_ref, sh_ref, w_ref, cos_ref, sin_ref,
                     q_ref, kvc_ref, kv_ref, gate_ref, *, tn):
    d = h_ref.shape[2]
    kv_w = NSA_KV_GROUPS * HEAD_DIM
    u = _norm_mod(h_ref[0], g_ref[...], sc_ref[0], sh_ref[0]).astype(BF16)
    cos_t = cos_ref[0]
    sin_t = sin_ref[0]
    n_q = d // tn
    n_kv = 6 * kv_w // tn
    for j in range(w_ref.shape[1] // tn):
        acc = _dot(u, w_ref[:, j * tn:(j + 1) * tn])
        if j < n_q:
            q = _rope(acc, cos_t, sin_t) * ATTN_SCALE
            q_ref[0, :, j * tn:(j + 1) * tn] = q.astype(BF16)
        elif j < n_q + n_kv:
            for part in range(tn // kv_w):
                i = (j - n_q) * (tn // kv_w) + part
                x = acc[:, part * kv_w:(part + 1) * kv_w]
                if i in (2, 4):
                    x = _rope(x, cos_t, sin_t)
                for grp in range(NSA_KV_GROUPS):
                    xg = x[:, grp * HEAD_DIM:(grp + 1) * HEAD_DIM]
                    if i < 2:
                        kvc_ref[0, i, grp] = xg
                    else:
                        kv_ref[0, i - 2, grp] = xg.astype(BF16)
        else:
            c0 = (j - n_q - n_kv) * tn
            gate_ref[0, :, c0:c0 + tn] = jax.nn.sigmoid(acc)


def _nsa_proj(h, g, sc, sh, w, cos_t, sin_t, tm=512, tn=256):
    bsz, s, d = h.shape
    n = w.shape[1]
    n_gate = NSA_KV_GROUPS * LANES
    return pl.pallas_call(
        functools.partial(_nsa_proj_kernel, tn=tn),
        out_shape=(jax.ShapeDtypeStruct((bsz, s, d), BF16),
                   jax.ShapeDtypeStruct((bsz, 2, NSA_KV_GROUPS, s, HEAD_DIM), F32),
                   jax.ShapeDtypeStruct((bsz, 4, NSA_KV_GROUPS, s, HEAD_DIM), BF16),
                   jax.ShapeDtypeStruct((bsz, s, n_gate), F32)),
        grid=(bsz, s // tm),
        in_specs=_proj_in_specs(tm, d, n) + [
            pl.BlockSpec((1, tm, LANES), lambda b, i: (b, i, 0)),
            pl.BlockSpec((1, tm, LANES), lambda b, i: (b, i, 0))],
        out_specs=(pl.BlockSpec((1, tm, d), lambda b, i: (b, i, 0)),
                   pl.BlockSpec((1, 2, NSA_KV_GROUPS, tm, HEAD_DIM), lambda b, i: (b, 0, 0, i, 0)),
                   pl.BlockSpec((1, 4, NSA_KV_GROUPS, tm, HEAD_DIM), lambda b, i: (b, 0, 0, i, 0)),
                   pl.BlockSpec((1, tm, n_gate), lambda b, i: (b, i, 0))),
        compiler_params=_cparams(2),
        name="nsa_proj",
    )(h, g, sc, sh, w, cos_t, sin_t)


def _cv_proj_kernel(h_ref, g_ref, sc_ref, sh_ref, w_ref, b_ref, o_ref, *, tn):
    d = h_ref.shape[2]
    u = _norm_mod(h_ref[0], g_ref[...], sc_ref[0], sh_ref[0]).astype(BF16)
    for j in range(d // tn):
        a = _dot(u, w_ref[:, j * tn:(j + 1) * tn]) + b_ref[:, j * tn:(j + 1) * tn]
        gt = _dot(u, w_ref[:, d + j * tn:d + (j + 1) * tn]) + b_ref[:, d + j * tn:d + (j + 1) * tn]
        o_ref[0, :, j * tn:(j + 1) * tn] = a * jax.nn.sigmoid(gt)


def _cv_proj(h, g, sc, sh, w, bias, tm=512, tn=256):
    bsz, s, d = h.shape
    n = w.shape[1]
    return pl.pallas_call(
        functools.partial(_cv_proj_kernel, tn=tn),
        out_shape=jax.ShapeDtypeStruct((bsz, s, d), F32),
        grid=(bsz, s // tm),
        in_specs=_proj_in_specs(tm, d, n) + [pl.BlockSpec((1, n), lambda b, i: (0, 0))],
        out_specs=pl.BlockSpec((1, tm, d), lambda b, i: (b, i, 0)),
        compiler_params=_cparams(2),
        name="cv_proj",
    )(h, g, sc, sh, w, bias)


def _out_proj_kernel(a_ref, w_ref, b_ref, h_ref, gate_ref, pg_ref, o_ref, *, tn):
    a = a_ref[0]
    d = w_ref.shape[1]
    for j in range(d // tn):
        o_ref[0, :, j * tn:(j + 1) * tn] = (_dot(a, w_ref[:, j * tn:(j + 1) * tn])
                                            + b_ref[:, j * tn:(j + 1) * tn])
    y = o_ref[0]
    o_ref[0] = h_ref[0] + gate_ref[0] * (_rms(y) * pg_ref[...])


def _out_proj(a, w, bias, h, gate, post_g, tm=512, tn=256):
    bsz, s, k = a.shape
    d = w.shape[1]
    return pl.pallas_call(
        functools.partial(_out_proj_kernel, tn=tn),
        out_shape=jax.ShapeDtypeStruct((bsz, s, d), F32),
        grid=(bsz, s // tm),
        in_specs=[pl.BlockSpec((1, tm, k), lambda b, i: (b, i, 0)),
                  pl.BlockSpec((k, d), lambda b, i: (0, 0)),
                  pl.BlockSpec((1, d), lambda b, i: (0, 0)),
                  pl.BlockSpec((1, tm, d), lambda b, i: (b, i, 0)),
                  pl.BlockSpec((1, 1, d), lambda b, i: (b, 0, 0)),
                  pl.BlockSpec((1, d), lambda b, i: (0, 0))],
        out_specs=pl.BlockSpec((1, tm, d), lambda b, i: (b, i, 0)),
        compiler_params=_cparams(2),
        name="out_proj",
    )(a, w, bias, h, gate, post_g)


def _mlp_kernel(h_ref, g_ref, sc_ref, sh_ref, w1_ref, w2_ref, gate_ref, pg_ref, o_ref, hid_ref,
                *, tf, tn):
    h = h_ref[0]
    u = _norm_mod(h, g_ref[...], sc_ref[0], sh_ref[0]).astype(BF16)
    d_ff = w1_ref.shape[1]
    d = w2_ref.shape[1]
    for c in range(d_ff // tf):
        a = jnp.maximum(_dot(u, w1_ref[:, c * tf:(c + 1) * tf]), 0.0)
        hid_ref[:, c * tf:(c + 1) * tf] = (a * a).astype(BF16)
    for j in range(d // tn):
        o_ref[0, :, j * tn:(j + 1) * tn] = _dot(hid_ref[...], w2_ref[:, j * tn:(j + 1) * tn])
    y = o_ref[0]
    o_ref[0] = h + gate_ref[0] * (_rms(y) * pg_ref[...])


def _mlp(h, g, sc, sh, w1, w2, gate, post_g, tm=512, tf=512, tn=256):
    bsz, s, d = h.shape
    d_ff = w1.shape[1]
    return pl.pallas_call(
        functools.partial(_mlp_kernel, tf=tf, tn=tn),
        out_shape=jax.ShapeDtypeStruct((bsz, s, d), F32),
        grid=(bsz, s // tm),
        in_specs=[pl.BlockSpec((1, tm, d), lambda b, i: (b, i, 0)),
                  pl.BlockSpec((1, d), lambda b, i: (0, 0)),
                  pl.BlockSpec((1, 1, d), lambda b, i: (b, 0, 0)),
                  pl.BlockSpec((1, 1, d), lambda b, i: (b, 0, 0)),
                  pl.BlockSpec((d, d_ff), lambda b, i: (0, 0), pipeline_mode=pl.Buffered(1)),
                  pl.BlockSpec((d_ff, d), lambda b, i: (0, 0), pipeline_mode=pl.Buffered(1)),
                  pl.BlockSpec((1, 1, d), lambda b, i: (b, 0, 0)),
                  pl.BlockSpec((1, d), lambda b, i: (0, 0))],
        out_specs=pl.BlockSpec((1, tm, d), lambda b, i: (b, i, 0)),
        scratch_shapes=[pltpu.VMEM((tm, d_ff), BF16)],
        compiler_params=_cparams(2),
        name="mlp",
    )(h, g, sc, sh, w1, w2, gate, post_g)


SB_ZERO_WEIGHT_LOG = -110.0


def _sb_tile(qh, k, v, tri2, carry, acc, strict):
    w = _dot_nt(qh, k)
    lk = jnp.minimum(w, 0.0) - jnp.log(1.0 + jnp.exp(-jnp.abs(w)))
    if strict is not None:
        lk = jnp.where(strict, lk, 0.0)
    hi, lo = _split_bf16(lk)
    tail = _dot(jnp.concatenate([hi, lo], axis=1), tri2)
    a = jnp.exp((lk - w) + tail + carry)
    if strict is not None:
        a = jnp.where(strict, a, 0.0)
    acc = acc + _dot(a.astype(BF16), v)
    carry = carry + jnp.sum(lk, axis=-1, keepdims=True)
    return carry, acc


def _sb_attn_kernel(q_ref, k_ref, v_ref, tri_ref, o_ref, *, t):
    qi = pl.program_id(2)
    q = q_ref[0]
    tri2 = tri_ref[...]
    lane = lax.broadcasted_iota(jnp.int32, (t, LANES), 1)
    row = lax.broadcasted_iota(jnp.int32, (t, t), 0)
    col = lax.broadcasted_iota(jnp.int32, (t, t), 1)
    strict = col < row
    zero_q = jnp.zeros_like(q)
    q_heads = (jnp.where(lane < HEAD_DIM, q, zero_q),
               jnp.where(lane >= HEAD_DIM, q, zero_q))

    def tiles(kj, state, mask):
        start = pl.multiple_of(kj * t, t)
        k = k_ref[0, pl.ds(start, t), :]
        v = v_ref[0, pl.ds(start, t), :]
        c0, a0 = _sb_tile(q_heads[0], k, v, tri2, state[0], state[1], mask)
        c1, a1 = _sb_tile(q_heads[1], k, v, tri2, state[2], state[3], mask)
        return c0, a0, c1, a1

    def largest(state):
        return jnp.max(jnp.maximum(state[0], state[2]))

    zeros = (jnp.zeros((t, 1), F32), jnp.zeros((t, LANES), F32))
    state = tiles(qi, zeros + zeros, strict)

    def cond(loop):
        return (loop[0] < qi) & (loop[1] > SB_ZERO_WEIGHT_LOG)

    def body(loop):
        state = tiles(qi - 1 - loop[0], loop[2:], None)
        return (loop[0] + 1, largest(state)) + state

    out = lax.while_loop(cond, body, (jnp.int32(0), largest(state)) + state)
    o_ref[0] = jnp.where(lane < HEAD_DIM, out[3], out[5]).astype(BF16)


def _sb_attention(qkv, t=256):
    bsz, s, n3 = qkv.shape
    d = n3 // 3
    n_pairs = d // LANES
    t = min(t, s)
    tri = np.tril(np.ones((t, t), np.float32), -1)
    tri = jnp.asarray(np.concatenate([tri, tri], axis=0), BF16)
    return pl.pallas_call(
        functools.partial(_sb_attn_kernel, t=t),
        out_shape=jax.ShapeDtypeStruct((bsz, s, d), BF16),
        grid=(bsz, n_pairs, s // t),
        in_specs=[pl.BlockSpec((1, t, LANES), lambda b, p, i: (b, i, p)),
                  pl.BlockSpec((1, s, LANES), lambda b, p, i: (b, 0, n_pairs + p)),
                  pl.BlockSpec((1, s, LANES), lambda b, p, i: (b, 0, 2 * n_pairs + p)),
                  pl.BlockSpec((2 * t, t), lambda b, p, i: (0, 0))],
        out_specs=pl.BlockSpec((1, t, LANES), lambda b, p, i: (b, i, p)),
        compiler_params=_cparams(3),
        name="sb_attention",
    )(qkv, qkv, qkv, tri)


def _compress_kernel(x_ref, pe_ref, w1_ref, w2_ref, cos_ref, sin_ref, o_ref, *, rope):
    w1 = w1_ref[...]
    hid = w1.shape[1] // 2
    n_seg = o_ref.shape[2]
    pre = jnp.zeros((n_seg, 2 * hid), F32)
    for tok in range(CMP_STRIDE):
        x_tok = x_ref[0, 0, 0, pl.ds(tok, n_seg, stride=CMP_STRIDE), :].astype(BF16)
        pre = pre + _dot(x_tok, w1[tok * HEAD_DIM:(tok + 1) * HEAD_DIM, :])
    pe_term = _dot(pe_ref[...], w1)
    bias = pe_term[0:1, :hid] + pe_term[8:9, hid:]
    nxt = pltpu.roll(pre[:, hid:], n_seg - 1, 0)
    mid = jax.nn.gelu(pre[:, :hid] + nxt + bias)
    out = _dot(mid.astype(BF16), w2_ref[...])
    if rope:
        out = _rope(out, cos_ref[0], sin_ref[0])
    o_ref[0, 0] = out[:, :HEAD_DIM].astype(BF16)


def _compress(kvc, which, pe, w1, w2, cos_c, sin_c, rope):
    bsz, _, grp, s, _ = kvc.shape
    n_seg = s // CMP_STRIDE
    hid = w1.shape[1]
    half = w1.shape[0] // 2
    w1cat = jnp.concatenate([w1[:half], w1[half:]], axis=1).astype(BF16)
    w2p = jnp.pad(w2, ((0, 0), (0, LANES - HEAD_DIM))).astype(BF16)
    pe_flat = pe.reshape(2, half)
    pe_rows = jnp.zeros((16, half), F32).at[0].set(pe_flat[0]).at[8].set(pe_flat[1]).astype(BF16)
    return pl.pallas_call(
        functools.partial(_compress_kernel, rope=rope),
        out_shape=jax.ShapeDtypeStruct((bsz, grp, n_seg, HEAD_DIM), BF16),
        grid=(bsz, grp),
        in_specs=[pl.BlockSpec((1, 1, 1, s, HEAD_DIM), lambda b, g: (b, which, g, 0, 0)),
                  pl.BlockSpec((16, half), lambda b, g: (0, 0)),
                  pl.BlockSpec((half, 2 * hid), lambda b, g: (0, 0)),
                  pl.BlockSpec((hid, LANES), lambda b, g: (0, 0)),
                  pl.BlockSpec((1, n_seg, LANES), lambda b, g: (b, 0, 0)),
                  pl.BlockSpec((1, n_seg, LANES), lambda b, g: (b, 0, 0))],
        out_specs=pl.BlockSpec((1, 1, n_seg, HEAD_DIM), lambda b, g: (b, g, 0, 0)),
        compiler_params=_cparams(2),
        name="nsa_compress",
    )(kvc, pe_rows, w1cat, w2p, cos_c, sin_c)


def _softmax_partial(q4, k, v, madd, r_heads):
    rows, tk = q4.shape[0], k.shape[0]
    s = _dot_nt(q4, k).reshape(r_heads, rows // r_heads, tk) + madd[None]
    s = s.reshape(rows, tk)
    m = jnp.max(s, axis=-1, keepdims=True)
    p = jnp.exp(s - m)
    return m, jnp.sum(p, axis=-1, keepdims=True), _dot(p.astype(BF16), v)


def _softmax_merge(a, b):
    m = jnp.maximum(a[0], b[0])
    fa = jnp.exp(a[0] - m)
    fb = jnp.exp(b[0] - m)
    return m, fa * a[1] + fb * b[1], fa * a[2] + fb * b[2]


def _nsa_attn_kernel(q_ref, kc_ref, vc_ref, ks_ref, vs_ref, kw_ref, vw_ref, gate_ref,
                     ovt_ref, eye_ref, exp_ref, o_ref, *, tq, tk, n_cmp, n_sel, n_win):
    qi = pl.program_id(2)
    r_heads = NSA_Q_PER_GROUP
    rows = r_heads * tq
    t0 = qi * tq
    n_cp = kc_ref.shape[2]
    n_slc = ovt_ref.shape[0]

    qf = q_ref[0].astype(F32)
    q4 = jnp.concatenate([qf[:, r * HEAD_DIM:(r + 1) * HEAD_DIM] for r in range(r_heads)],
                         axis=0).astype(BF16)

    kc = kc_ref[0, 0]
    vc = vc_ref[0, 0]
    sc = _dot_nt(q4, kc).reshape(r_heads, tq, n_cp)
    t_c = t0 + lax.broadcasted_iota(jnp.int32, (tq, n_cp), 0)
    n_c = lax.broadcasted_iota(jnp.int32, (tq, n_cp), 1)
    cmp_ok = (n_c * CMP_STRIDE + (CMP_LEN - 1) <= t_c) & (n_c < n_cmp)
    any_ok = (t_c[:, 0:1] >= CMP_LEN - 1).astype(F32)
    sc = jnp.where(cmp_ok[None], sc, NEG_INF)
    e = jnp.exp(sc - jnp.max(sc, axis=-1, keepdims=True))
    p_cmp = e / jnp.sum(e, axis=-1, keepdims=True) * any_ok[None]
    o_cmp = _dot(p_cmp.reshape(rows, n_cp).astype(BF16), vc)

    p_grp = jnp.sum(p_cmp, axis=0)
    p_hi, p_lo = _split_bf16(p_grp)
    ovt = ovt_ref[...]
    imp_t = _dot_nt(ovt, p_hi) + _dot_nt(ovt, p_lo)
    blk = lax.broadcasted_iota(jnp.int32, (n_slc, tq), 0)
    tok = t0 + lax.broadcasted_iota(jnp.int32, (n_slc, tq), 1)
    cur = lax.shift_right_logical(tok, SLC_LEN.bit_length() - 1)
    forced = (blk == 0) | (blk == cur) | (blk == cur - 1)
    score = jnp.where(blk * SLC_LEN <= tok, jnp.where(forced, FORCE_BONUS, imp_t), NEG_INF)
    rank = jnp.zeros((n_slc, tq), jnp.int32)
    for i in range(n_slc):
        s_i = score[i:i + 1, :]
        ahead = (s_i > score) | ((s_i == score) & (blk > i))
        rank = rank + jnp.where(ahead, 1, 0)
    neg_t = jnp.where(rank < n_sel, 0.0, NEG_INF).astype(BF16)
    neg = _dot_nt(eye_ref[...], neg_t).astype(BF16)

    t_k = t0 + lax.broadcasted_iota(jnp.int32, (tq, tk), 0)
    c_k = lax.broadcasted_iota(jnp.int32, (tq, tk), 1)

    def slc_tile(kj, causal, extra=None):
        start = pl.multiple_of(kj * tk, tk)
        k = ks_ref[0, 0, pl.ds(start, tk), :]
        v = vs_ref[0, 0, pl.ds(start, tk), :]
        madd = _dot(neg, exp_ref[kj])
        if causal:
            madd = jnp.where(c_k + kj * tk <= t_k, madd, NEG_INF)
        if extra is not None:
            madd = madd + extra
        return _softmax_partial(q4, k, v, madd, r_heads)

    last = (t0 + tq - 1) // tk
    init = (jnp.full((rows, 1), NEG_INF, F32), jnp.zeros((rows, 1), F32),
            jnp.zeros((rows, HEAD_DIM), F32))

    def slc_pair(i, state):
        return _softmax_merge(state, _softmax_merge(slc_tile(2 * i, False), slc_tile(2 * i + 1, False)))

    state = lax.fori_loop(0, lax.shift_right_logical(last, 1), slc_pair, init)
    odd_extra = jnp.where((last & 1) == 1, 0.0, NEG_INF)
    state = _softmax_merge(state, slc_tile(last, True))
    state = _softmax_merge(state, slc_tile(jnp.maximum(last - 1, 0), False, odd_extra))
    o_slc = state[2] / state[1]

    def win_tile(kj):
        start = pl.multiple_of(jnp.maximum(kj, 0) * tk, tk)
        k = kw_ref[0, 0, pl.ds(start, tk), :]
        v = vw_ref[0, 0, pl.ds(start, tk), :]
        key = c_k + kj * tk
        ok = (key <= t_k) & (key > t_k - WINDOW) & (key >= 0)
        return _softmax_partial(q4, k, v, jnp.where(ok, 0.0, NEG_INF), r_heads)

    win = win_tile(last)
    for back in range(1, n_win):
        win = _softmax_merge(win, win_tile(last - back))
    o_win = win[2] / win[1]

    gates = gate_ref[0]
    pieces = []
    for r in range(r_heads):
        sl = slice(r * tq, (r + 1) * tq)
        pieces.append(gates[:, r:r + 1] * o_cmp[sl]
                      + gates[:, r_heads + r:r_heads + r + 1] * o_slc[sl]
                      + gates[:, 2 * r_heads + r:2 * r_heads + r + 1] * o_win[sl])
    o_ref[0] = jnp.concatenate(pieces, axis=1).astype(BF16)


def _nsa_attention(q, kv4, k_cmp, v_cmp, gates, tq=128, tk=256):
    bsz, s, d = q.shape
    grp = NSA_KV_GROUPS
    gw = NSA_Q_PER_GROUP * HEAD_DIM
    n_cp = k_cmp.shape[2]
    n_cmp = (s - CMP_LEN) // CMP_STRIDE + 1
    n_slc = s // SLC_LEN
    n_sel = min(N_SELECT, n_slc)
    tq = min(tq, s)
    tk = min(tk, s)
    c0 = np.arange(n_cp)[None, :] * CMP_STRIDE
    s0 = np.arange(n_slc)[:, None] * SLC_LEN
    ov = np.maximum(np.minimum(c0 + CMP_LEN, s0 + SLC_LEN) - np.maximum(c0, s0), 0) / CMP_LEN
    ov = ov * (np.arange(n_cp)[None, :] < n_cmp)
    ovt = jnp.asarray(ov, BF16)
    eye = jnp.asarray(np.eye(tq, dtype=np.float32), BF16)
    n_kt = s // tk
    expand = (np.arange(n_slc)[None, :, None]
              == (np.arange(n_kt)[:, None, None] * (tk // SLC_LEN) + np.arange(tk)[None, None, :] // SLC_LEN))
    expand = jnp.asarray(expand.astype(np.float32), BF16)

    n_win = max((t0 + tq - 1) // tk - (t0 - WINDOW + 1) // tk + 1 for t0 in range(0, tk, tq))

    def kv_spec(i):
        return pl.BlockSpec((1, 1, 1, s, HEAD_DIM), lambda b, g, t, i=i: (b, i, g, 0, 0))

    def kernel(q_ref, kc_ref, vc_ref, ks_ref, vs_ref, kw_ref, vw_ref, *rest):
        return _nsa_attn_kernel(q_ref, kc_ref, vc_ref, ks_ref.at[0], vs_ref.at[0], kw_ref.at[0],
                                vw_ref.at[0], *rest, tq=tq, tk=tk, n_cmp=n_cmp, n_sel=n_sel,
                                n_win=n_win)

    return pl.pallas_call(
        kernel,
        out_shape=jax.ShapeDtypeStruct((bsz, s, d), BF16),
        grid=(bsz, grp, s // tq),
        in_specs=[pl.BlockSpec((1, tq, gw), lambda b, g, t: (b, t, g)),
                  pl.BlockSpec((1, 1, n_cp, HEAD_DIM), lambda b, g, t: (b, g, 0, 0)),
                  pl.BlockSpec((1, 1, n_cp, HEAD_DIM), lambda b, g, t: (b, g, 0, 0)),
                  kv_spec(0), kv_spec(1), kv_spec(2), kv_spec(3),
                  pl.BlockSpec((1, tq, LANES), lambda b, g, t: (b, t, g)),
                  pl.BlockSpec((n_slc, n_cp), lambda b, g, t: (0, 0)),
                  pl.BlockSpec((tq, tq), lambda b, g, t: (0, 0)),
                  pl.BlockSpec((n_kt, n_slc, tk), lambda b, g, t: (0, 0, 0))],
        out_specs=pl.BlockSpec((1, tq, gw), lambda b, g, t: (b, t, g)),
        compiler_params=_cparams(3),
        name="nsa_attention",
    )(q, k_cmp, v_cmp, kv4, kv4, kv4, kv4, gates, ovt, eye, expand)


_HALO = 32


def _dwconv_kernel(x_ref, prev_ref, dw_ref, dwb_ref, lng_ref, lnb_ref, o_ref, ext_ref, *, tm):
    i = pl.program_id(1)
    prev = prev_ref[0]
    ext_ref[0:_HALO, :] = jnp.where(i > 0, prev, jnp.zeros_like(prev))
    ext_ref[_HALO:_HALO + tm, :] = x_ref[0]
    off = _HALO - (CONV_WIDTH - 1)
    acc = jnp.zeros(x_ref.shape[1:], F32) + dwb_ref[...]
    for k in range(CONV_WIDTH):
        acc = acc + ext_ref[off + k:off + k + tm, :] * dw_ref[k:k + 1, :]
    mu = jnp.mean(acc, axis=-1, keepdims=True)
    cen = acc - mu
    var = jnp.mean(cen * cen, axis=-1, keepdims=True)
    y = cen * lax.rsqrt(var + NORM_EPS) * lng_ref[...] + lnb_ref[...]
    o_ref[0] = (y * jax.nn.sigmoid(y)).astype(BF16)


def _dwconv_ln_swish(x, dw, dw_b, ln_g, ln_b, tm=256):
    bsz, s, d = x.shape
    tm = min(tm, s)
    per = tm // _HALO
    return pl.pallas_call(
        functools.partial(_dwconv_kernel, tm=tm),
        out_shape=jax.ShapeDtypeStruct((bsz, s, d), BF16),
        grid=(bsz, s // tm),
        in_specs=[pl.BlockSpec((1, tm, d), lambda b, i: (b, i, 0)),
                  pl.BlockSpec((1, _HALO, d), lambda b, i: (b, jnp.maximum(i * per - 1, 0), 0)),
                  pl.BlockSpec((_HALO, d), lambda b, i: (0, 0)),
                  pl.BlockSpec((1, d), lambda b, i: (0, 0)),
                  pl.BlockSpec((1, d), lambda b, i: (0, 0)),
                  pl.BlockSpec((1, d), lambda b, i: (0, 0))],
        out_specs=pl.BlockSpec((1, tm, d), lambda b, i: (b, i, 0)),
        scratch_shapes=[pltpu.VMEM((_HALO + tm, d), F32)],
        compiler_params=_cparams(2),
        name="dwconv_ln_swish",
    )(x, x, dw, dw_b, ln_g, ln_b)


def _rope_tables(positions):
    half = ROPE_DIM // 2
    inv_freq = ROPE_THETA ** (-jnp.arange(half, dtype=F32) / half)
    ang = positions.astype(F32)[..., None] * inv_freq
    cos, sin = jnp.cos(ang), jnp.sin(ang)
    ones = jnp.ones(ang.shape[:-1] + (HEAD_DIM - ROPE_DIM,), F32)
    cos_h = jnp.concatenate([cos, cos, ones], axis=-1)
    sin_h = jnp.concatenate([-sin, sin, 0.0 * ones], axis=-1)
    reps = LANES // HEAD_DIM
    return jnp.concatenate([cos_h] * reps, axis=-1), jnp.concatenate([sin_h] * reps, axis=-1)


def _nsa_weight(w_in, d):
    kv_end = d + 6 * NSA_KV_GROUPS * HEAD_DIM
    wg = w_in[:, kv_end:].reshape(d, NSA_KV_GROUPS, NSA_Q_PER_GROUP, 3)
    wg = wg.transpose(0, 1, 3, 2).reshape(d, NSA_KV_GROUPS, 3 * NSA_Q_PER_GROUP)
    wg = jnp.pad(wg, ((0, 0), (0, 0), (0, LANES - 3 * NSA_Q_PER_GROUP)))
    return jnp.concatenate([w_in[:, :kv_end], wg.reshape(d, NSA_KV_GROUPS * LANES)], axis=1).astype(BF16)


def kernel(x, c, positions, ada_w, ada_b, mix_pre_g, mix_post_g, ffn_pre_g, ffn_post_g, ffn_w1, ffn_w2, sb_w_in, sb_w_out, nsa_w_in, nsa_w_out, nsa_pe_k, nsa_w1_k, nsa_w2_k, nsa_pe_v, nsa_w1_v, nsa_w2_v, cv_w_in, cv_b_in, cv_dw, cv_dw_b, cv_ln_g, cv_ln_b, cv_w_out, cv_b_out):
    bsz, s, d = x.shape
    depth = ada_w.shape[0]
    n_mixers = 3
    mod = _ada_mod(c, ada_w, ada_b).reshape(depth, bsz, 6, 1, d)
    zero_bias = jnp.zeros((1, d), F32)
    h = x
    for i in range(depth):
        sh1, sc1, g1, sh2, sc2, g2 = [mod[i, :, m] for m in range(6)]
        pre_g = mix_pre_g[i].reshape(1, d)
        post_g = mix_post_g[i].reshape(1, d)
        kind, j = i % n_mixers, i // n_mixers
        if kind == 0:
            qkv = _sb_proj(h, pre_g, sc1, sh1, sb_w_in[j].astype(BF16))
            o = _sb_attention(qkv)
            h = _out_proj(o, sb_w_out[j].astype(BF16), zero_bias, h, g1, post_g)
        elif kind == 1:
            cos_t, sin_t = _rope_tables(positions)
            q, kvc, kv4, gates = _nsa_proj(h, pre_g, sc1, sh1, _nsa_weight(nsa_w_in[j], d), cos_t, sin_t)
            end = jnp.minimum(jnp.arange(s // CMP_STRIDE) * CMP_STRIDE + CMP_LEN - 1, s - 1)
            cos_c, sin_c = _rope_tables(positions[:, end])
            k_cmp = _compress(kvc, 0, nsa_pe_k[j], nsa_w1_k[j], nsa_w2_k[j], cos_c, sin_c, True)
            v_cmp = _compress(kvc, 1, nsa_pe_v[j], nsa_w1_v[j], nsa_w2_v[j], cos_c, sin_c, False)
            o = _nsa_attention(q, kv4, k_cmp, v_cmp, gates)
            h = _out_proj(o, nsa_w_out[j].astype(BF16), zero_bias, h, g1, post_g)
        else:
            a = _cv_proj(h, pre_g, sc1, sh1, cv_w_in[j].astype(BF16), cv_b_in[j].reshape(1, 2 * d))
            dw = jnp.pad(cv_dw[j].reshape(CONV_WIDTH, d), ((0, _HALO - CONV_WIDTH), (0, 0)))
            a = _dwconv_ln_swish(a, dw, cv_dw_b[j].reshape(1, d), cv_ln_g[j].reshape(1, d),
                                 cv_ln_b[j].reshape(1, d))
            h = _out_proj(a, cv_w_out[j].astype(BF16), cv_b_out[j].reshape(1, d), h, g1, post_g)
        h = _mlp(h, ffn_pre_g[i].reshape(1, d), sc2, sh2, ffn_w1[i].astype(BF16),
                 ffn_w2[i].astype(BF16), g2, ffn_post_g[i].reshape(1, d))
    return h
```

```python
import functools

import numpy as np
import jax
import jax.numpy as jnp
from jax import lax
from jax.experimental import pallas as pl
from jax.experimental.pallas import tpu as pltpu

F32 = jnp.float32
BF16 = jnp.bfloat16

N_HEADS = 16
HEAD_DIM = 64
ROPE_THETA = 500000.0
ROPE_DIM = HEAD_DIM // 4
NORM_EPS = 1e-6
NSA_KV_GROUPS = 4
NSA_Q_PER_GROUP = N_HEADS // NSA_KV_GROUPS
CMP_LEN = 32
CMP_STRIDE = 16
SLC_LEN = 64
N_SELECT = 16
WINDOW = 512
FORCE_BONUS = 1e4
NEG_INF = -1e30
CONV_WIDTH = 31
ATTN_SCALE = HEAD_DIM ** -0.5

LANES = 128
VMEM_LIMIT_BYTES = 56 * 1024 * 1024

_NT = (((1,), (1,)), ((), ()))


def _cparams(n_axes):
    return pltpu.CompilerParams(dimension_semantics=("arbitrary",) * n_axes,
                                vmem_limit_bytes=VMEM_LIMIT_BYTES)


def _dot(a, b):
    return jnp.dot(a, b, preferred_element_type=F32)


def _dot_nt(a, b):
    return lax.dot_general(a, b, _NT, preferred_element_type=F32)


def _split_bf16(x):
    hi = x.astype(BF16)
    lo = (x - hi.astype(F32)).astype(BF16)
    return hi, lo


def _rms(x):
    return x * lax.rsqrt(jnp.mean(x * x, axis=-1, keepdims=True) + NORM_EPS)


def _norm_mod(h, g, sc, sh):
    return (_rms(h) * g) * (1.0 + sc) + sh


def _rope(x, cos_t, sin_t):
    rows, w = x.shape
    reps = w // LANES
    cos_w = jnp.concatenate([cos_t] * reps, axis=1) if reps > 1 else cos_t
    sin_w = jnp.concatenate([sin_t] * reps, axis=1) if reps > 1 else sin_t
    half = ROPE_DIM // 2
    lane = lax.broadcasted_iota(jnp.int32, (rows, w), 1)
    first_half = (lane & (HEAD_DIM - 1)) < half
    partner = jnp.where(first_half, pltpu.roll(x, w - half, 1), pltpu.roll(x, half, 1))
    return x * cos_w + partner * sin_w


def _ada_kernel(c_ref, w_ref, b_ref, o_ref):
    c = c_ref[...]
    cond = c * jax.nn.sigmoid(c)
    o_ref[0] = jnp.dot(cond, w_ref[0], preferred_element_type=F32,
                       precision=lax.Precision.HIGHEST) + b_ref[0]


def _ada_mod(c, ada_w, ada_b):
    depth, d, n = ada_w.shape
    b = c.shape[0]
    tn = 1024
    return pl.pallas_call(
        _ada_kernel,
        out_shape=jax.ShapeDtypeStruct((depth, b, n), F32),
        grid=(depth, n // tn),
        in_specs=[pl.BlockSpec((b, d), lambda i, j: (0, 0)),
                  pl.BlockSpec((1, d, tn), lambda i, j: (i, 0, j)),
                  pl.BlockSpec((1, 1, tn), lambda i, j: (i, 0, j))],
        out_specs=pl.BlockSpec((1, b, tn), lambda i, j: (i, 0, j)),
        compiler_params=_cparams(2),
        name="ada_mod",
    )(c, ada_w, ada_b.reshape(depth, 1, n))


def _proj_in_specs(tm, d, n_w):
    return [pl.BlockSpec((1, tm, d), lambda b, i: (b, i, 0)),
            pl.BlockSpec((1, d), lambda b, i: (0, 0)),
            pl.BlockSpec((1, 1, d), lambda b, i: (b, 0, 0)),
            pl.BlockSpec((1, 1, d), lambda b, i: (b, 0, 0)),
            pl.BlockSpec((d, n_w), lambda b, i: (0, 0))]


def _sb_proj_kernel(h_ref, g_ref, sc_ref, sh_ref, w_ref, o_ref, *, tn):
    d = h_ref.shape[2]
    u = _norm_mod(h_ref[0], g_ref[...], sc_ref[0], sh_ref[0]).astype(BF16)
    for j in range(w_ref.shape[1] // tn):
        acc = _dot(u, w_ref[:, j * tn:(j + 1) * tn])
        if j * tn < d:
            acc = acc * (-ATTN_SCALE)
        o_ref[0, :, j * tn:(j + 1) * tn] = acc.astype(BF16)


def _sb_proj(h, g, sc, sh, w, tm=512, tn=512):
    bsz, s, d = h.shape
    n = w.shape[1]
    return pl.pallas_call(
        functools.partial(_sb_proj_kernel, tn=tn),
        out_shape=jax.ShapeDtypeStruct((bsz, s, n), BF16),
        grid=(bsz, s // tm),
        in_specs=_proj_in_specs(tm, d, n),
        out_specs=pl.BlockSpec((1, tm, n), lambda b, i: (b, i, 0)),
        compiler_params=_cparams(2),
        name="sb_proj",
    )(h, g, sc, sh, w)


def _nsa_proj_kernel(h_ref, g_ref, sc_ref, sh_ref, w_ref, cos_ref, sin_ref,
                     q_ref, kvc_ref, kv_ref, gate_ref, *, tn):
    d = h_ref.shape[2]
    kv_w = NSA_KV_GROUPS * HEAD_DIM
    u = _norm_mod(h_ref[0], g_ref[...], sc_ref[0], sh_ref[0]).astype(BF16)
    cos_t = cos_ref[0]
    sin_t = sin_ref[0]
    n_q = d // tn
    n_kv = 6 * kv_w // tn
    for j in range(w_ref.shape[1] // tn):
        acc = _dot(u, w_ref[:, j * tn:(j + 1) * tn])
        if j < n_q:
            q = _rope(acc, cos_t, sin_t) * ATTN_SCALE
            q_ref[0, :, j * tn:(j + 1) * tn] = q.astype(BF16)
        elif j < n_q + n_kv:
            for part in range(tn // kv_w):
                i = (j - n_q) * (tn // kv_w) + part
                x = acc[:, part * kv_w:(part + 1) * kv_w]
                if i in (2, 4):
                    x = _rope(x, cos_t, sin_t)
                for grp in range(NSA_KV_GROUPS):
                    xg = x[:, grp * HEAD_DIM:(grp + 1) * HEAD_DIM]
                    if i < 2:
                        kvc_ref[0, i, grp] = xg
                    else:
                        kv_ref[0, i - 2, grp] = xg.astype(BF16)
        else:
            c0 = (j - n_q - n_kv) * tn
            gate_ref[0, :, c0:c0 + tn] = jax.nn.sigmoid(acc)


def _nsa_proj(h, g, sc, sh, w, cos_t, sin_t, tm=512, tn=256):
    bsz, s, d = h.shape
    n = w.shape[1]
    n_gate = NSA_KV_GROUPS * LANES
    return pl.pallas_call(
        functools.partial(_nsa_proj_kernel, tn=tn),
        out_shape=(jax.ShapeDtypeStruct((bsz, s, d), BF16),
                   jax.ShapeDtypeStruct((bsz, 2, NSA_KV_GROUPS, s, HEAD_DIM), F32),
                   jax.ShapeDtypeStruct((bsz, 4, NSA_KV_GROUPS, s, HEAD_DIM), BF16),
                   jax.ShapeDtypeStruct((bsz, s, n_gate), F32)),
        grid=(bsz, s // tm),
        in_specs=_proj_in_specs(tm, d, n) + [
            pl.BlockSpec((1, tm, LANES), lambda b, i: (b, i, 0)),
            pl.BlockSpec((1, tm, LANES), lambda b, i: (b, i, 0))],
        out_specs=(pl.BlockSpec((1, tm, d), lambda b, i: (b, i, 0)),
                   pl.BlockSpec((1, 2, NSA_KV_GROUPS, tm, HEAD_DIM), lambda b, i: (b, 0, 0, i, 0)),
                   pl.BlockSpec((1, 4, NSA_KV_GROUPS, tm, HEAD_DIM), lambda b, i: (b, 0, 0, i, 0)),
                   pl.BlockSpec((1, tm, n_gate), lambda b, i: (b, i, 0))),
        compiler_params=_cparams(2),
        name="nsa_proj",
    )(h, g, sc, sh, w, cos_t, sin_t)


def _cv_proj_kernel(h_ref, g_ref, sc_ref, sh_ref, w_ref, b_ref, o_ref, *, tn):
    d = h_ref.shape[2]
    u = _norm_mod(h_ref[0], g_ref[...], sc_ref[0], sh_ref[0]).astype(BF16)
    for j in range(d // tn):
        a = _dot(u, w_ref[:, j * tn:(j + 1) * tn]) + b_ref[:, j * tn:(j + 1) * tn]
        gt = _dot(u, w_ref[:, d + j * tn:d + (j + 1) * tn]) + b_ref[:, d + j * tn:d + (j + 1) * tn]
        o_ref[0, :, j * tn:(j + 1) * tn] = a * jax.nn.sigmoid(gt)


def _cv_proj(h, g, sc, sh, w, bias, tm=512, tn=256):
    bsz, s, d = h.shape
    n = w.shape[1]
    return pl.pallas_call(
        functools.partial(_cv_proj_kernel, tn=tn),
        out_shape=jax.ShapeDtypeStruct((bsz, s, d), F32),
        grid=(bsz, s // tm),
        in_specs=_proj_in_specs(tm, d, n) + [pl.BlockSpec((1, n), lambda b, i: (0, 0))],
        out_specs=pl.BlockSpec((1, tm, d), lambda b, i: (b, i, 0)),
        compiler_params=_cparams(2),
        name="cv_proj",
    )(h, g, sc, sh, w, bias)


def _out_proj_kernel(a_ref, w_ref, b_ref, h_ref, gate_ref, pg_ref, o_ref, *, tn):
    a = a_ref[0]
    d = w_ref.shape[1]
    for j in range(d // tn):
        o_ref[0, :, j * tn:(j + 1) * tn] = (_dot(a, w_ref[:, j * tn:(j + 1) * tn])
                                            + b_ref[:, j * tn:(j + 1) * tn])
    y = o_ref[0]
    o_ref[0] = h_ref[0] + gate_ref[0] * (_rms(y) * pg_ref[...])


def _out_proj(a, w, bias, h, gate, post_g, tm=512, tn=256):
    bsz, s, k = a.shape
    d = w.shape[1]
    return pl.pallas_call(
        functools.partial(_out_proj_kernel, tn=tn),
        out_shape=jax.ShapeDtypeStruct((bsz, s, d), F32),
        grid=(bsz, s // tm),
        in_specs=[pl.BlockSpec((1, tm, k), lambda b, i: (b, i, 0)),
                  pl.BlockSpec((k, d), lambda b, i: (0, 0)),
                  pl.BlockSpec((1, d), lambda b, i: (0, 0)),
                  pl.BlockSpec((1, tm, d), lambda b, i: (b, i, 0)),
                  pl.BlockSpec((1, 1, d), lambda b, i: (b, 0, 0)),
                  pl.BlockSpec((1, d), lambda b, i: (0, 0))],
        out_specs=pl.BlockSpec((1, tm, d), lambda b, i: (b, i, 0)),
        compiler_params=_cparams(2),
        name="out_proj",
    )(a, w, bias, h, gate, post_g)


def _mlp_kernel(h_ref, g_ref, sc_ref, sh_ref, w1_ref, w2_ref, gate_ref, pg_ref, o_ref, hid_ref,
                *, tf, tn):
    h = h_ref[0]
    u = _norm_mod(h, g_ref[...], sc_ref[0], sh_ref[0]).astype(BF16)
    d_ff = w1_ref.shape[1]
    d = w2_ref.shape[1]
    for c in range(d_ff // tf):
        a = jnp.maximum(_dot(u, w1_ref[:, c * tf:(c + 1) * tf]), 0.0)
        hid_ref[:, c * tf:(c + 1) * tf] = (a * a).astype(BF16)
    for j in range(d // tn):
        o_ref[0, :, j * tn:(j + 1) * tn] = _dot(hid_ref[...], w2_ref[:, j * tn:(j + 1) * tn])
    y = o_ref[0]
    o_ref[0] = h + gate_ref[0] * (_rms(y) * pg_ref[...])


def _mlp(h, g, sc, sh, w1, w2, gate, post_g, tm=512, tf=512, tn=256):
    bsz, s, d = h.shape
    d_ff = w1.shape[1]
    return pl.pallas_call(
        functools.partial(_mlp_kernel, tf=tf, tn=tn),
        out_shape=jax.ShapeDtypeStruct((bsz, s, d), F32),
        grid=(bsz, s // tm),
        in_specs=[pl.BlockSpec((1, tm, d), lambda b, i: (b, i, 0)),
                  pl.BlockSpec((1, d), lambda b, i: (0, 0)),
                  pl.BlockSpec((1, 1, d), lambda b, i: (b, 0, 0)),
                  pl.BlockSpec((1, 1, d), lambda b, i: (b, 0, 0)),
                  pl.BlockSpec((d, d_ff), lambda b, i: (0, 0), pipeline_mode=pl.Buffered(1)),
                  pl.BlockSpec((d_ff, d), lambda b, i: (0, 0), pipeline_mode=pl.Buffered(1)),
                  pl.BlockSpec((1, 1, d), lambda b, i: (b, 0, 0)),
                  pl.BlockSpec((1, d), lambda b, i: (0, 0))],
        out_specs=pl.BlockSpec((1, tm, d), lambda b, i: (b, i, 0)),
        scratch_shapes=[pltpu.VMEM((tm, d_ff), BF16)],
        compiler_params=_cparams(2),
        name="mlp",
    )(h, g, sc, sh, w1, w2, gate, post_g)


SB_ZERO_WEIGHT_LOG = -110.0


def _sb_tiles(qs, ks, vs, tri2, state, strict):
    n = len(qs)
    ws = [_dot_nt(qs[h], ks[h]) for h in range(n)]
    lks, hls = [], []
    for w in ws:
        lk = jnp.minimum(w, 0.0) - jnp.log(1.0 + jnp.exp(-jnp.abs(w)))
        if strict is not None:
            lk = jnp.where(strict, lk, 0.0)
        hi, lo = _split_bf16(lk)
        lks.append(lk)
        hls.append(jnp.concatenate([hi, lo], axis=1))
    tails = [_dot(hl, tri2) for hl in hls]
    probs = []
    for h in range(n):
        a = jnp.exp((lks[h] - ws[h]) + tails[h] + state[2 * h])
        if strict is not None:
            a = jnp.where(strict, a, 0.0)
        probs.append(a.astype(BF16))
    out = ()
    for h in range(n):
        out += (state[2 * h] + jnp.sum(lks[h], axis=-1, keepdims=True),
                state[2 * h + 1] + _dot(probs[h], vs[h]))
    return out


def _sb_attn_kernel(q_ref, k_ref, v_ref, tri_ref, o_ref, *, t, pairs):
    qi = pl.program_id(2)
    tri2 = tri_ref[...]
    lane = lax.broadcasted_iota(jnp.int32, (t, LANES), 1)
    row = lax.broadcasted_iota(jnp.int32, (t, t), 0)
    col = lax.broadcasted_iota(jnp.int32, (t, t), 1)
    strict = col < row
    q_heads = []
    for p in range(pairs):
        q = q_ref[0, :, p * LANES:(p + 1) * LANES]
        zero_q = jnp.zeros_like(q)
        q_heads += [jnp.where(lane < HEAD_DIM, q, zero_q),
                    jnp.where(lane >= HEAD_DIM, q, zero_q)]
    n_heads = 2 * pairs

    def tiles(kj, state, mask):
        start = pl.multiple_of(kj * t, t)
        cols = [slice((hd // 2) * LANES, (hd // 2 + 1) * LANES) for hd in range(n_heads)]
        ks = [k_ref[0, pl.ds(start, t), c] for c in cols]
        vs = [v_ref[0, pl.ds(start, t), c] for c in cols]
        return _sb_tiles(q_heads, ks, vs, tri2, state, mask)

    def largest(state):
        carries = state[0::2]
        top = carries[0]
        for c in carries[1:]:
            top = jnp.maximum(top, c)
        return jnp.max(top)

    zeros = (jnp.zeros((t, 1), F32), jnp.zeros((t, LANES), F32))
    state = tiles(qi, zeros * n_heads, strict)

    def cond(loop):
        return (loop[0] < qi) & (loop[1] > SB_ZERO_WEIGHT_LOG)

    def body(loop):
        state = tiles(qi - 1 - loop[0], loop[2:], None)
        return (loop[0] + 1, largest(state)) + state

    out = lax.while_loop(cond, body, (jnp.int32(0), largest(state)) + state)[2:]
    for p in range(pairs):
        o_ref[0, :, p * LANES:(p + 1) * LANES] = jnp.where(
            lane < HEAD_DIM, out[4 * p + 1], out[4 * p + 3]).astype(BF16)


def _sb_attention(qkv, t=256, pairs=2):
    bsz, s, n3 = qkv.shape
    d = n3 // 3
    w = pairs * LANES
    n_grp = d // w
    t = min(t, s)
    tri = np.tril(np.ones((t, t), np.float32), -1)
    tri = jnp.asarray(np.concatenate([tri, tri], axis=0), BF16)
    return pl.pallas_call(
        functools.partial(_sb_attn_kernel, t=t, pairs=pairs),
        out_shape=jax.ShapeDtypeStruct((bsz, s, d), BF16),
        grid=(bsz, n_grp, s // t),
        in_specs=[pl.BlockSpec((1, t, w), lambda b, p, i: (b, i, p)),
                  pl.BlockSpec((1, s, w), lambda b, p, i: (b, 0, n_grp + p)),
                  pl.BlockSpec((1, s, w), lambda b, p, i: (b, 0, 2 * n_grp + p)),
                  pl.BlockSpec((2 * t, t), lambda b, p, i: (0, 0))],
        out_specs=pl.BlockSpec((1, t, w), lambda b, p, i: (b, i, p)),
        compiler_params=_cparams(3),
        name="sb_attention",
    )(qkv, qkv, qkv, tri)


def _compress_kernel(x_ref, pe_ref, w1_ref, w2_ref, cos_ref, sin_ref, o_ref, *, rope):
    w1 = w1_ref[...]
    hid = w1.shape[1] // 2
    n_seg = o_ref.shape[2]
    pre = jnp.zeros((n_seg, 2 * hid), F32)
    for tok in range(CMP_STRIDE):
        x_tok = x_ref[0, 0, 0, pl.ds(tok, n_seg, stride=CMP_STRIDE), :].astype(BF16)
        pre = pre + _dot(x_tok, w1[tok * HEAD_DIM:(tok + 1) * HEAD_DIM, :])
    pe_term = _dot(pe_ref[...], w1)
    bias = pe_term[0:1, :hid] + pe_term[8:9, hid:]
    nxt = pltpu.roll(pre[:, hid:], n_seg - 1, 0)
    mid = jax.nn.gelu(pre[:, :hid] + nxt + bias)
    out = _dot(mid.astype(BF16), w2_ref[...])
    if rope:
        out = _rope(out, cos_ref[0], sin_ref[0])
    o_ref[0, 0] = out[:, :HEAD_DIM].astype(BF16)


def _compress(kvc, which, pe, w1, w2, cos_c, sin_c, rope):
    bsz, _, grp, s, _ = kvc.shape
    n_seg = s // CMP_STRIDE
    hid = w1.shape[1]
    half = w1.shape[0] // 2
    w1cat = jnp.concatenate([w1[:half], w1[half:]], axis=1).astype(BF16)
    w2p = jnp.pad(w2, ((0, 0), (0, LANES - HEAD_DIM))).astype(BF16)
    pe_flat = pe.reshape(2, half)
    pe_rows = jnp.zeros((16, half), F32).at[0].set(pe_flat[0]).at[8].set(pe_flat[1]).astype(BF16)
    return pl.pallas_call(
        functools.partial(_compress_kernel, rope=rope),
        out_shape=jax.ShapeDtypeStruct((bsz, grp, n_seg, HEAD_DIM), BF16),
        grid=(bsz, grp),
        in_specs=[pl.BlockSpec((1, 1, 1, s, HEAD_DIM), lambda b, g: (b, which, g, 0, 0)),
                  pl.BlockSpec((16, half), lambda b, g: (0, 0)),
                  pl.BlockSpec((half, 2 * hid), lambda b, g: (0, 0)),
                  pl.BlockSpec((hid, LANES), lambda b, g: (0, 0)),
                  pl.BlockSpec((1, n_seg, LANES), lambda b, g: (b, 0, 0)),
                  pl.BlockSpec((1, n_seg, LANES), lambda b, g: (b, 0, 0))],
        out_specs=pl.BlockSpec((1, 1, n_seg, HEAD_DIM), lambda b, g: (b, g, 0, 0)),
        compiler_params=_cparams(2),
        name="nsa_compress",
    )(kvc, pe_rows, w1cat, w2p, cos_c, sin_c)


def _masked_scores(q4, k, madd, r_heads):
    rows, tk = q4.shape[0], k.shape[0]
    s = _dot_nt(q4, k).reshape(r_heads, rows // r_heads, tk) + madd[None]
    return s.reshape(rows, tk)


def _online_softmax(s, v, m, l, acc):
    m_new = jnp.maximum(m, jnp.max(s, axis=-1, keepdims=True))
    alpha = jnp.exp(m - m_new)
    p = jnp.exp(s - m_new)
    l = alpha * l + jnp.sum(p, axis=-1, keepdims=True)
    acc = alpha * acc + _dot(p.astype(BF16), v)
    return m_new, l, acc


def _nsa_attn_kernel(q_ref, kc_ref, vc_ref, ks_ref, vs_ref, kw_ref, vw_ref, gate_ref,
                     ovt_ref, eye_ref, exp_ref, o_ref, *, tq, tk, wk, n_cmp, n_sel):
    qi = pl.program_id(2)
    r_heads = NSA_Q_PER_GROUP
    rows = r_heads * tq
    t0 = qi * tq
    n_cp = kc_ref.shape[2]
    n_slc = ovt_ref.shape[0]

    qf = q_ref[0].astype(F32)
    q4 = jnp.concatenate([qf[:, r * HEAD_DIM:(r + 1) * HEAD_DIM] for r in range(r_heads)],
                         axis=0).astype(BF16)

    kc = kc_ref[0, 0]
    vc = vc_ref[0, 0]
    sc = _dot_nt(q4, kc).reshape(r_heads, tq, n_cp)
    t_c = t0 + lax.broadcasted_iota(jnp.int32, (tq, n_cp), 0)
    n_c = lax.broadcasted_iota(jnp.int32, (tq, n_cp), 1)
    cmp_ok = (n_c * CMP_STRIDE + (CMP_LEN - 1) <= t_c) & (n_c < n_cmp)
    any_ok = (t_c[:, 0:1] >= CMP_LEN - 1).astype(F32)
    sc = jnp.where(cmp_ok[None], sc, NEG_INF)
    e = jnp.exp(sc - jnp.max(sc, axis=-1, keepdims=True))
    p_cmp = e * (any_ok[None] / jnp.sum(e, axis=-1, keepdims=True))
    o_cmp = _dot(p_cmp.reshape(rows, n_cp).astype(BF16), vc)

    p_grp = jnp.sum(p_cmp, axis=0)
    p_hi, p_lo = _split_bf16(p_grp)
    ovt = ovt_ref[...]
    imp_t = _dot_nt(ovt, p_hi) + _dot_nt(ovt, p_lo)
    blk = lax.broadcasted_iota(jnp.int32, (n_slc, tq), 0)
    tok = t0 + lax.broadcasted_iota(jnp.int32, (n_slc, tq), 1)
    cur = lax.shift_right_logical(tok, SLC_LEN.bit_length() - 1)
    forced = (blk == 0) | (blk == cur) | (blk == cur - 1)
    score = jnp.where(blk * SLC_LEN <= tok, jnp.where(forced, FORCE_BONUS, imp_t), NEG_INF)
    rank = jnp.zeros((n_slc, tq), jnp.int32)
    for i in range(n_slc):
        s_i = score[i:i + 1, :]
        ahead = (s_i > score) | ((s_i == score) & (blk > i))
        rank = rank + jnp.where(ahead, 1, 0)
    neg_t = jnp.where(rank < n_sel, 0.0, NEG_INF).astype(BF16)
    neg = _dot_nt(eye_ref[...], neg_t).astype(BF16)

    t_k = t0 + lax.broadcasted_iota(jnp.int32, (tq, tk), 0)
    c_k = lax.broadcasted_iota(jnp.int32, (tq, tk), 1)

    def slc_scores(kj):
        start = pl.multiple_of(kj * tk, tk)
        madd = _dot(neg, exp_ref[kj])
        madd = jnp.where(c_k + kj * tk <= t_k, madd, NEG_INF)
        return _masked_scores(q4, ks_ref[0, 0, pl.ds(start, tk), :], madd, r_heads)

    def slc_values(kj):
        return vs_ref[0, 0, pl.ds(pl.multiple_of(kj * tk, tk), tk), :]

    last = (t0 + tq - 1) // tk
    init = (jnp.full((rows, 1), NEG_INF, F32), jnp.zeros((rows, 1), F32),
            jnp.zeros((rows, HEAD_DIM), F32))

    def slc_step(kj, carry):
        s_next = slc_scores(kj + 1)
        return _online_softmax(carry[3], slc_values(kj), *carry[:3]) + (s_next,)

    carry = lax.fori_loop(0, last, slc_step, init + (slc_scores(0),))

    start_w = pl.multiple_of(jnp.maximum(t0 + tq - wk, 0), tq)
    t_w = t0 + lax.broadcasted_iota(jnp.int32, (tq, wk), 0)
    key_w = start_w + lax.broadcasted_iota(jnp.int32, (tq, wk), 1)
    ok_w = (key_w <= t_w) & (key_w > t_w - WINDOW)
    s_win = _masked_scores(q4, kw_ref[0, 0, pl.ds(start_w, wk), :], jnp.where(ok_w, 0.0, NEG_INF), r_heads)
    _, l_s, acc_s = _online_softmax(carry[3], slc_values(last), *carry[:3])
    _, l_w, acc_w = _online_softmax(s_win, vw_ref[0, 0, pl.ds(start_w, wk), :], *init)
    o_slc = acc_s / l_s
    o_win = acc_w / l_w

    gates = gate_ref[0]
    pieces = []
    for r in range(r_heads):
        sl = slice(r * tq, (r + 1) * tq)
        pieces.append(gates[:, r:r + 1] * o_cmp[sl]
                      + gates[:, r_heads + r:r_heads + r + 1] * o_slc[sl]
                      + gates[:, 2 * r_heads + r:2 * r_heads + r + 1] * o_win[sl])
    o_ref[0] = jnp.concatenate(pieces, axis=1).astype(BF16)


def _nsa_attention(q, kv4, k_cmp, v_cmp, gates, tq=128, tk=512):
    bsz, s, d = q.shape
    grp = NSA_KV_GROUPS
    gw = NSA_Q_PER_GROUP * HEAD_DIM
    n_cp = k_cmp.shape[2]
    n_cmp = (s - CMP_LEN) // CMP_STRIDE + 1
    n_slc = s // SLC_LEN
    n_sel = min(N_SELECT, n_slc)
    tq = min(tq, s)
    tk = min(tk, s)
    c0 = np.arange(n_cp)[None, :] * CMP_STRIDE
    s0 = np.arange(n_slc)[:, None] * SLC_LEN
    ov = np.maximum(np.minimum(c0 + CMP_LEN, s0 + SLC_LEN) - np.maximum(c0, s0), 0) / CMP_LEN
    ov = ov * (np.arange(n_cp)[None, :] < n_cmp)
    ovt = jnp.asarray(ov, BF16)
    eye = jnp.asarray(np.eye(tq, dtype=np.float32), BF16)
    n_kt = s // tk
    expand = (np.arange(n_slc)[None, :, None]
              == (np.arange(n_kt)[:, None, None] * (tk // SLC_LEN) + np.arange(tk)[None, None, :] // SLC_LEN))
    expand = jnp.asarray(expand.astype(np.float32), BF16)

    wk = min(-(-(WINDOW + tq - 1) // tq) * tq, s)

    def kv_spec(i):
        return pl.BlockSpec((1, 1, 1, s, HEAD_DIM), lambda b, g, t, i=i: (b, i, g, 0, 0))

    def kernel(q_ref, kc_ref, vc_ref, ks_ref, vs_ref, kw_ref, vw_ref, *rest):
        return _nsa_attn_kernel(q_ref, kc_ref, vc_ref, ks_ref.at[0], vs_ref.at[0], kw_ref.at[0],
                                vw_ref.at[0], *rest, tq=tq, tk=tk, wk=wk, n_cmp=n_cmp, n_sel=n_sel)

    return pl.pallas_call(
        kernel,
        out_shape=jax.ShapeDtypeStruct((bsz, s, d), BF16),
        grid=(bsz, grp, s // tq),
        in_specs=[pl.BlockSpec((1, tq, gw), lambda b, g, t: (b, t, g)),
                  pl.BlockSpec((1, 1, n_cp, HEAD_DIM), lambda b, g, t: (b, g, 0, 0)),
                  pl.BlockSpec((1, 1, n_cp, HEAD_DIM), lambda b, g, t: (b, g, 0, 0)),
                  kv_spec(0), kv_spec(1), kv_spec(2), kv_spec(3),
                  pl.BlockSpec((1, tq, LANES), lambda b, g, t: (b, t, g)),
                  pl.BlockSpec((n_slc, n_cp), lambda b, g, t: (0, 0)),
                  pl.BlockSpec((tq, tq), lambda b, g, t: (0, 0)),
                  pl.BlockSpec((n_kt, n_slc, tk), lambda b, g, t: (0, 0, 0))],
        out_specs=pl.BlockSpec((1, tq, gw), lambda b, g, t: (b, t, g)),
        compiler_params=_cparams(3),
        name="nsa_attention",
    )(q, k_cmp, v_cmp, kv4, kv4, kv4, kv4, gates, ovt, eye, expand)


_HALO = 32


def _dwconv_kernel(x_ref, prev_ref, dw_ref, dwb_ref, lng_ref, lnb_ref, o_ref, ext_ref, *, tm):
    i = pl.program_id(1)
    prev = prev_ref[0]
    ext_ref[0:_HALO, :] = jnp.where(i > 0, prev, jnp.zeros_like(prev))
    ext_ref[_HALO:_HALO + tm, :] = x_ref[0]
    off = _HALO - (CONV_WIDTH - 1)
    acc = jnp.zeros(x_ref.shape[1:], F32) + dwb_ref[...]
    for k in range(CONV_WIDTH):
        acc = acc + ext_ref[off + k:off + k + tm, :] * dw_ref[k:k + 1, :]
    mu = jnp.mean(acc, axis=-1, keepdims=True)
    cen = acc - mu
    var = jnp.mean(cen * cen, axis=-1, keepdims=True)
    y = cen * lax.rsqrt(var + NORM_EPS) * lng_ref[...] + lnb_ref[...]
    o_ref[0] = (y * jax.nn.sigmoid(y)).astype(BF16)


def _dwconv_ln_swish(x, dw, dw_b, ln_g, ln_b, tm=256):
    bsz, s, d = x.shape
    tm = min(tm, s)
    per = tm // _HALO
    return pl.pallas_call(
        functools.partial(_dwconv_kernel, tm=tm),
        out_shape=jax.ShapeDtypeStruct((bsz, s, d), BF16),
        grid=(bsz, s // tm),
        in_specs=[pl.BlockSpec((1, tm, d), lambda b, i: (b, i, 0)),
                  pl.BlockSpec((1, _HALO, d), lambda b, i: (b, jnp.maximum(i * per - 1, 0), 0)),
                  pl.BlockSpec((_HALO, d), lambda b, i: (0, 0)),
                  pl.BlockSpec((1, d), lambda b, i: (0, 0)),
                  pl.BlockSpec((1, d), lambda b, i: (0, 0)),
                  pl.BlockSpec((1, d), lambda b, i: (0, 0))],
        out_specs=pl.BlockSpec((1, tm, d), lambda b, i: (b, i, 0)),
        scratch_shapes=[pltpu.VMEM((_HALO + tm, d), F32)],
        compiler_params=_cparams(2),
        name="dwconv_ln_swish",
    )(x, x, dw, dw_b, ln_g, ln_b)


def _rope_tables(positions):
    half = ROPE_DIM // 2
    inv_freq = ROPE_THETA ** (-jnp.arange(half, dtype=F32) / half)
    ang = positions.astype(F32)[..., None] * inv_freq
    cos, sin = jnp.cos(ang), jnp.sin(ang)
    ones = jnp.ones(ang.shape[:-1] + (HEAD_DIM - ROPE_DIM,), F32)
    cos_h = jnp.concatenate([cos, cos, ones], axis=-1)
    sin_h = jnp.concatenate([-sin, sin, 0.0 * ones], axis=-1)
    reps = LANES // HEAD_DIM
    return jnp.concatenate([cos_h] * reps, axis=-1), jnp.concatenate([sin_h] * reps, axis=-1)


def _nsa_weight(w_in, d):
    kv_end = d + 6 * NSA_KV_GROUPS * HEAD_DIM
    wg = w_in[:, kv_end:].reshape(d, NSA_KV_GROUPS, NSA_Q_PER_GROUP, 3)
    wg = wg.transpose(0, 1, 3, 2).reshape(d, NSA_KV_GROUPS, 3 * NSA_Q_PER_GROUP)
    wg = jnp.pad(wg, ((0, 0), (0, 0), (0, LANES - 3 * NSA_Q_PER_GROUP)))
    return jnp.concatenate([w_in[:, :kv_end], wg.reshape(d, NSA_KV_GROUPS * LANES)], axis=1).astype(BF16)


def kernel(x, c, positions, ada_w, ada_b, mix_pre_g, mix_post_g, ffn_pre_g, ffn_post_g, ffn_w1, ffn_w2, sb_w_in, sb_w_out, nsa_w_in, nsa_w_out, nsa_pe_k, nsa_w1_k, nsa_w2_k, nsa_pe_v, nsa_w1_v, nsa_w2_v, cv_w_in, cv_b_in, cv_dw, cv_dw_b, cv_ln_g, cv_ln_b, cv_w_out, cv_b_out):
    bsz, s, d = x.shape
    depth = ada_w.shape[0]
    n_mixers = 3
    mod = _ada_mod(c, ada_w, ada_b).reshape(depth, bsz, 6, 1, d)
    zero_bias = jnp.zeros((1, d), F32)
    h = x
    for i in range(depth):
        sh1, sc1, g1, sh2, sc2, g2 = [mod[i, :, m] for m in range(6)]
        pre_g = mix_pre_g[i].reshape(1, d)
        post_g = mix_post_g[i].reshape(1, d)
        kind, j = i % n_mixers, i // n_mixers
        if kind == 0:
            qkv = _sb_proj(h, pre_g, sc1, sh1, sb_w_in[j].astype(BF16))
            o = _sb_attention(qkv)
            h = _out_proj(o, sb_w_out[j].astype(BF16), zero_bias, h, g1, post_g)
        elif kind == 1:
            cos_t, sin_t = _rope_tables(positions)
            q, kvc, kv4, gates = _nsa_proj(h, pre_g, sc1, sh1, _nsa_weight(nsa_w_in[j], d), cos_t, sin_t)
            end = jnp.minimum(jnp.arange(s // CMP_STRIDE) * CMP_STRIDE + CMP_LEN - 1, s - 1)
            cos_c, sin_c = _rope_tables(positions[:, end])
            k_cmp = _compress(kvc, 0, nsa_pe_k[j], nsa_w1_k[j], nsa_w2_k[j], cos_c, sin_c, True)
            v_cmp = _compress(kvc, 1, nsa_pe_v[j], nsa_w1_v[j], nsa_w2_v[j], cos_c, sin_c, False)
            o = _nsa_attention(q, kv4, k_cmp, v_cmp, gates)
            h = _out_proj(o, nsa_w_out[j].astype(BF16), zero_bias, h, g1, post_g)
        else:
            a = _cv_proj(h, pre_g, sc1, sh1, cv_w_in[j].astype(BF16), cv_b_in[j].reshape(1, 2 * d))
            dw = jnp.pad(cv_dw[j].reshape(CONV_WIDTH, d), ((0, _HALO - CONV_WIDTH), (0, 0)))
            a = _dwconv_ln_swish(a, dw, cv_dw_b[j].reshape(1, d), cv_ln_g[j].reshape(1, d),
                                 cv_ln_b[j].reshape(1, d))
            h = _out_proj(a, cv_w_out[j].astype(BF16), cv_b_out[j].reshape(1, d), h, g1, post_g)
        h = _mlp(h, ffn_pre_g[i].reshape(1, d), sc2, sh2, ffn_w1[i].astype(BF16),
                 ffn_w2[i].astype(BF16), g2, ffn_post_g[i].reshape(1, d))
    return h
```

```python
import functools

import numpy as np
import jax
import jax.numpy as jnp
from jax import lax
from jax.experimental import pallas as pl
from jax.experimental.pallas import tpu as pltpu

F32 = jnp.float32
BF16 = jnp.bfloat16

N_HEADS = 16
HEAD_DIM = 64
ROPE_THETA = 500000.0
ROPE_DIM = HEAD_DIM // 4
NORM_EPS = 1e-6
NSA_KV_GROUPS = 4
NSA_Q_PER_GROUP = N_HEADS // NSA_KV_GROUPS
CMP_LEN = 32
CMP_STRIDE = 16
SLC_LEN = 64
N_SELECT = 16
WINDOW = 512
FORCE_BONUS = 1e4
NEG_INF = -1e30
CONV_WIDTH = 31
ATTN_SCALE = HEAD_DIM ** -0.5
LOG2_E = 1.4426950408889634

LANES = 128
VMEM_LIMIT_BYTES = 56 * 1024 * 1024

_NT = (((1,), (1,)), ((), ()))


def _cparams(n_axes):
    return pltpu.CompilerParams(dimension_semantics=("arbitrary",) * n_axes,
                                vmem_limit_bytes=VMEM_LIMIT_BYTES)


def _dot(a, b):
    return jnp.dot(a, b, preferred_element_type=F32)


def _dot_nt(a, b):
    return lax.dot_general(a, b, _NT, preferred_element_type=F32)


def _split_bf16(x):
    hi = x.astype(BF16)
    lo = (x - hi.astype(F32)).astype(BF16)
    return hi, lo


def _rms(x):
    return x * lax.rsqrt(jnp.mean(x * x, axis=-1, keepdims=True) + NORM_EPS)


def _norm_mod(h, g, sc, sh):
    return (_rms(h) * g) * (1.0 + sc) + sh


def _rope(x, cos_t, sin_t):
    rows, w = x.shape
    reps = w // LANES
    cos_w = jnp.concatenate([cos_t] * reps, axis=1) if reps > 1 else cos_t
    sin_w = jnp.concatenate([sin_t] * reps, axis=1) if reps > 1 else sin_t
    half = ROPE_DIM // 2
    lane = lax.broadcasted_iota(jnp.int32, (rows, w), 1)
    first_half = (lane & (HEAD_DIM - 1)) < half
    partner = jnp.where(first_half, pltpu.roll(x, w - half, 1), pltpu.roll(x, half, 1))
    return x * cos_w + partner * sin_w


def _ada_kernel(c_ref, w_ref, b_ref, o_ref):
    c = c_ref[...]
    cond = c * jax.nn.sigmoid(c)
    o_ref[0] = jnp.dot(cond, w_ref[0], preferred_element_type=F32,
                       precision=lax.Precision.HIGHEST) + b_ref[0]


def _ada_mod(c, ada_w, ada_b):
    depth, d, n = ada_w.shape
    b = c.shape[0]
    tn = 1024
    return pl.pallas_call(
        _ada_kernel,
        out_shape=jax.ShapeDtypeStruct((depth, b, n), F32),
        grid=(depth, n // tn),
        in_specs=[pl.BlockSpec((b, d), lambda i, j: (0, 0)),
                  pl.BlockSpec((1, d, tn), lambda i, j: (i, 0, j)),
                  pl.BlockSpec((1, 1, tn), lambda i, j: (i, 0, j))],
        out_specs=pl.BlockSpec((1, b, tn), lambda i, j: (i, 0, j)),
        compiler_params=_cparams(2),
        name="ada_mod",
    )(c, ada_w, ada_b.reshape(depth, 1, n))


def _proj_in_specs(tm, d, n_w):
    return [pl.BlockSpec((1, tm, d), lambda b, i: (b, i, 0)),
            pl.BlockSpec((1, d), lambda b, i: (0, 0)),
            pl.BlockSpec((1, 1, d), lambda b, i: (b, 0, 0)),
            pl.BlockSpec((1, 1, d), lambda b, i: (b, 0, 0)),
            pl.BlockSpec((d, n_w), lambda b, i: (0, 0))]


def _sb_proj_kernel(h_ref, g_ref, sc_ref, sh_ref, w_ref, o_ref, *, tn):
    d = h_ref.shape[2]
    u = _norm_mod(h_ref[0], g_ref[...], sc_ref[0], sh_ref[0]).astype(BF16)
    for j in range(w_ref.shape[1] // tn):
        acc = _dot(u, w_ref[:, j * tn:(j + 1) * tn])
        if j * tn < d:
            acc = acc * (-ATTN_SCALE)
        o_ref[0, :, j * tn:(j + 1) * tn] = acc.astype(BF16)


def _sb_proj(h, g, sc, sh, w, tm=512, tn=512):
    bsz, s, d = h.shape
    n = w.shape[1]
    return pl.pallas_call(
        functools.partial(_sb_proj_kernel, tn=tn),
        out_shape=jax.ShapeDtypeStruct((bsz, s, n), BF16),
        grid=(bsz, s // tm),
        in_specs=_proj_in_specs(tm, d, n),
        out_specs=pl.BlockSpec((1, tm, n), lambda b, i: (b, i, 0)),
        compiler_params=_cparams(2),
        name="sb_proj",
    )(h, g, sc, sh, w)


def _nsa_proj_kernel(h_ref, g_ref, sc_ref, sh_ref, w_ref, cos_ref, sin_ref,
                     q_ref, kvc_ref, kk_ref, vv_ref, gate_ref, *, tn):
    d = h_ref.shape[2]
    tm = h_ref.shape[1]
    kv_w = NSA_KV_GROUPS * HEAD_DIM
    u = _norm_mod(h_ref[0], g_ref[...], sc_ref[0], sh_ref[0]).astype(BF16)
    cos_t = cos_ref[0]
    sin_t = sin_ref[0]
    one_col = jnp.where(lax.broadcasted_iota(jnp.int32, (tm, LANES - HEAD_DIM), 1) == 0, 1.0, 0.0)
    n_q = d // tn
    n_kv = 6 * kv_w // tn
    for j in range(w_ref.shape[1] // tn):
        acc = _dot(u, w_ref[:, j * tn:(j + 1) * tn])
        if j < n_q:
            q = _rope(acc, cos_t, sin_t) * ATTN_SCALE
            q_ref[0, :, j * tn:(j + 1) * tn] = q.astype(BF16)
        elif j < n_q + n_kv:
            for part in range(tn // kv_w):
                i = (j - n_q) * (tn // kv_w) + part
                x = acc[:, part * kv_w:(part + 1) * kv_w]
                if i in (2, 4):
                    x = _rope(x, cos_t, sin_t)
                for grp in range(NSA_KV_GROUPS):
                    xg = x[:, grp * HEAD_DIM:(grp + 1) * HEAD_DIM]
                    if i < 2:
                        kvc_ref[0, i, grp] = xg
                    elif i in (2, 4):
                        kk_ref[0, (i - 2) // 2, grp] = xg.astype(BF16)
                    else:
                        vv_ref[0, (i - 3) // 2, grp] = jnp.concatenate([xg, one_col], axis=1).astype(BF16)
        else:
            c0 = (j - n_q - n_kv) * tn
            gate_ref[0, :, c0:c0 + tn] = jax.nn.sigmoid(acc)


def _nsa_proj(h, g, sc, sh, w, cos_t, sin_t, tm=512, tn=256):
    bsz, s, d = h.shape
    n = w.shape[1]
    n_gate = NSA_KV_GROUPS * LANES
    return pl.pallas_call(
        functools.partial(_nsa_proj_kernel, tn=tn),
        out_shape=(jax.ShapeDtypeStruct((bsz, s, d), BF16),
                   jax.ShapeDtypeStruct((bsz, 2, NSA_KV_GROUPS, s, HEAD_DIM), F32),
                   jax.ShapeDtypeStruct((bsz, 2, NSA_KV_GROUPS, s, HEAD_DIM), BF16),
                   jax.ShapeDtypeStruct((bsz, 2, NSA_KV_GROUPS, s, LANES), BF16),
                   jax.ShapeDtypeStruct((bsz, s, n_gate), F32)),
        grid=(bsz, s // tm),
        in_specs=_proj_in_specs(tm, d, n) + [
            pl.BlockSpec((1, tm, LANES), lambda b, i: (b, i, 0)),
            pl.BlockSpec((1, tm, LANES), lambda b, i: (b, i, 0))],
        out_specs=(pl.BlockSpec((1, tm, d), lambda b, i: (b, i, 0)),
                   pl.BlockSpec((1, 2, NSA_KV_GROUPS, tm, HEAD_DIM), lambda b, i: (b, 0, 0, i, 0)),
                   pl.BlockSpec((1, 2, NSA_KV_GROUPS, tm, HEAD_DIM), lambda b, i: (b, 0, 0, i, 0)),
                   pl.BlockSpec((1, 2, NSA_KV_GROUPS, tm, LANES), lambda b, i: (b, 0, 0, i, 0)),
                   pl.BlockSpec((1, tm, n_gate), lambda b, i: (b, i, 0))),
        compiler_params=_cparams(2),
        name="nsa_proj",
    )(h, g, sc, sh, w, cos_t, sin_t)


def _cv_proj_kernel(h_ref, g_ref, sc_ref, sh_ref, w_ref, b_ref, o_ref, *, tn):
    d = h_ref.shape[2]
    u = _norm_mod(h_ref[0], g_ref[...], sc_ref[0], sh_ref[0]).astype(BF16)
    for j in range(d // tn):
        a = _dot(u, w_ref[:, j * tn:(j + 1) * tn]) + b_ref[:, j * tn:(j + 1) * tn]
        gt = _dot(u, w_ref[:, d + j * tn:d + (j + 1) * tn]) + b_ref[:, d + j * tn:d + (j + 1) * tn]
        o_ref[0, :, j * tn:(j + 1) * tn] = a * jax.nn.sigmoid(gt)


def _cv_proj(h, g, sc, sh, w, bias, tm=512, tn=256):
    bsz, s, d = h.shape
    n = w.shape[1]
    return pl.pallas_call(
        functools.partial(_cv_proj_kernel, tn=tn),
        out_shape=jax.ShapeDtypeStruct((bsz, s, d), F32),
        grid=(bsz, s // tm),
        in_specs=_proj_in_specs(tm, d, n) + [pl.BlockSpec((1, n), lambda b, i: (0, 0))],
        out_specs=pl.BlockSpec((1, tm, d), lambda b, i: (b, i, 0)),
        compiler_params=_cparams(2),
        name="cv_proj",
    )(h, g, sc, sh, w, bias)


def _out_proj_kernel(a_ref, w_ref, b_ref, h_ref, gate_ref, pg_ref, o_ref, *, tn):
    a = a_ref[0]
    d = w_ref.shape[1]
    for j in range(d // tn):
        o_ref[0, :, j * tn:(j + 1) * tn] = (_dot(a, w_ref[:, j * tn:(j + 1) * tn])
                                            + b_ref[:, j * tn:(j + 1) * tn])
    y = o_ref[0]
    o_ref[0] = h_ref[0] + gate_ref[0] * (_rms(y) * pg_ref[...])


def _out_proj(a, w, bias, h, gate, post_g, tm=512, tn=256):
    bsz, s, k = a.shape
    d = w.shape[1]
    return pl.pallas_call(
        functools.partial(_out_proj_kernel, tn=tn),
        out_shape=jax.ShapeDtypeStruct((bsz, s, d), F32),
        grid=(bsz, s // tm),
        in_specs=[pl.BlockSpec((1, tm, k), lambda b, i: (b, i, 0)),
                  pl.BlockSpec((k, d), lambda b, i: (0, 0)),
                  pl.BlockSpec((1, d), lambda b, i: (0, 0)),
                  pl.BlockSpec((1, tm, d), lambda b, i: (b, i, 0)),
                  pl.BlockSpec((1, 1, d), lambda b, i: (b, 0, 0)),
                  pl.BlockSpec((1, d), lambda b, i: (0, 0))],
        out_specs=pl.BlockSpec((1, tm, d), lambda b, i: (b, i, 0)),
        compiler_params=_cparams(2),
        name="out_proj",
    )(a, w, bias, h, gate, post_g)


def _mlp_kernel(h_ref, g_ref, sc_ref, sh_ref, w1_ref, w2_ref, gate_ref, pg_ref, o_ref, hid_ref,
                *, tf, tn):
    h = h_ref[0]
    u = _norm_mod(h, g_ref[...], sc_ref[0], sh_ref[0]).astype(BF16)
    d_ff = w1_ref.shape[1]
    d = w2_ref.shape[1]
    for c in range(d_ff // tf):
        a = jnp.maximum(_dot(u, w1_ref[:, c * tf:(c + 1) * tf]), 0.0)
        hid_ref[:, c * tf:(c + 1) * tf] = (a * a).astype(BF16)
    for j in range(d // tn):
        o_ref[0, :, j * tn:(j + 1) * tn] = _dot(hid_ref[...], w2_ref[:, j * tn:(j + 1) * tn])
    y = o_ref[0]
    o_ref[0] = h + gate_ref[0] * (_rms(y) * pg_ref[...])


def _mlp(h, g, sc, sh, w1, w2, gate, post_g, tm=512, tf=512, tn=256):
    bsz, s, d = h.shape
    d_ff = w1.shape[1]
    return pl.pallas_call(
        functools.partial(_mlp_kernel, tf=tf, tn=tn),
        out_shape=jax.ShapeDtypeStruct((bsz, s, d), F32),
        grid=(bsz, s // tm),
        in_specs=[pl.BlockSpec((1, tm, d), lambda b, i: (b, i, 0)),
                  pl.BlockSpec((1, d), lambda b, i: (0, 0)),
                  pl.BlockSpec((1, 1, d), lambda b, i: (b, 0, 0)),
                  pl.BlockSpec((1, 1, d), lambda b, i: (b, 0, 0)),
                  pl.BlockSpec((d, d_ff), lambda b, i: (0, 0), pipeline_mode=pl.Buffered(1)),
                  pl.BlockSpec((d_ff, d), lambda b, i: (0, 0), pipeline_mode=pl.Buffered(1)),
                  pl.BlockSpec((1, 1, d), lambda b, i: (b, 0, 0)),
                  pl.BlockSpec((1, d), lambda b, i: (0, 0))],
        out_specs=pl.BlockSpec((1, tm, d), lambda b, i: (b, i, 0)),
        scratch_shapes=[pltpu.VMEM((tm, d_ff), BF16)],
        compiler_params=_cparams(2),
        name="mlp",
    )(h, g, sc, sh, w1, w2, gate, post_g)


SB_ZERO_WEIGHT_LOG = -110.0


def _sb_tiles(qs, ks, vs, tri2, state, strict):
    n = len(qs)
    ws = [_dot_nt(qs[h], ks[h]) for h in range(n)]
    lks, hls = [], []
    for w in ws:
        lk = jnp.minimum(w, 0.0) - jnp.log(1.0 + jnp.exp2(jnp.abs(w) * (-LOG2_E)))
        if strict is not None:
            lk = jnp.where(strict, lk, 0.0)
        hi, lo = _split_bf16(lk)
        lks.append(lk)
        hls.append(jnp.concatenate([hi, lo], axis=1))
    tails = [_dot(hl, tri2) for hl in hls]
    probs = []
    for h in range(n):
        a = jnp.exp((lks[h] - ws[h]) + tails[h] + state[2 * h])
        if strict is not None:
            a = jnp.where(strict, a, 0.0)
        probs.append(a.astype(BF16))
    out = ()
    for h in range(n):
        out += (state[2 * h] + jnp.sum(lks[h], axis=-1, keepdims=True),
                state[2 * h + 1] + _dot(probs[h], vs[h]))
    return out


def _sb_attn_kernel(q_ref, k_ref, v_ref, tri_ref, o_ref, *, t, pairs):
    qi = pl.program_id(2)
    tri2 = tri_ref[...]
    lane = lax.broadcasted_iota(jnp.int32, (t, LANES), 1)
    row = lax.broadcasted_iota(jnp.int32, (t, t), 0)
    col = lax.broadcasted_iota(jnp.int32, (t, t), 1)
    strict = col < row
    q_heads = []
    for p in range(pairs):
        q = q_ref[0, :, p * LANES:(p + 1) * LANES]
        zero_q = jnp.zeros_like(q)
        q_heads += [jnp.where(lane < HEAD_DIM, q, zero_q),
                    jnp.where(lane >= HEAD_DIM, q, zero_q)]
    n_heads = 2 * pairs

    def tiles(kj, state, mask):
        start = pl.multiple_of(kj * t, t)
        cols = [slice((hd // 2) * LANES, (hd // 2 + 1) * LANES) for hd in range(n_heads)]
        ks = [k_ref[0, pl.ds(start, t), c] for c in cols]
        vs = [v_ref[0, pl.ds(start, t), c] for c in cols]
        return _sb_tiles(q_heads, ks, vs, tri2, state, mask)

    def largest(state):
        carries = state[0::2]
        top = carries[0]
        for c in carries[1:]:
            top = jnp.maximum(top, c)
        return jnp.max(top)

    zeros = (jnp.zeros((t, 1), F32), jnp.zeros((t, LANES), F32))
    state = tiles(qi, zeros * n_heads, strict)

    def cond(loop):
        return (loop[0] < qi) & (loop[1] > SB_ZERO_WEIGHT_LOG)

    def body(loop):
        state = tiles(qi - 1 - loop[0], loop[2:], None)
        return (loop[0] + 1, largest(state)) + state

    out = lax.while_loop(cond, body, (jnp.int32(0), largest(state)) + state)[2:]
    for p in range(pairs):
        o_ref[0, :, p * LANES:(p + 1) * LANES] = jnp.where(
            lane < HEAD_DIM, out[4 * p + 1], out[4 * p + 3]).astype(BF16)


def _sb_attention(qkv, t=256, pairs=2):
    bsz, s, n3 = qkv.shape
    d = n3 // 3
    w = pairs * LANES
    n_grp = d // w
    t = min(t, s)
    tri = np.tril(np.ones((t, t), np.float32), -1)
    tri = jnp.asarray(np.concatenate([tri, tri], axis=0), BF16)
    return pl.pallas_call(
        functools.partial(_sb_attn_kernel, t=t, pairs=pairs),
        out_shape=jax.ShapeDtypeStruct((bsz, s, d), BF16),
        grid=(bsz, n_grp, s // t),
        in_specs=[pl.BlockSpec((1, t, w), lambda b, p, i: (b, i, p)),
                  pl.BlockSpec((1, s, w), lambda b, p, i: (b, 0, n_grp + p)),
                  pl.BlockSpec((1, s, w), lambda b, p, i: (b, 0, 2 * n_grp + p)),
                  pl.BlockSpec((2 * t, t), lambda b, p, i: (0, 0))],
        out_specs=pl.BlockSpec((1, t, w), lambda b, p, i: (b, i, p)),
        compiler_params=_cparams(3),
        name="sb_attention",
    )(qkv, qkv, qkv, tri)


def _compress_kernel(x_ref, pe_ref, w1_ref, w2_ref, cos_ref, sin_ref, o_ref, *, rope):
    w1 = w1_ref[...]
    hid = w1.shape[1] // 2
    n_seg = o_ref.shape[2]
    pre = jnp.zeros((n_seg, 2 * hid), F32)
    for tok in range(CMP_STRIDE):
        x_tok = x_ref[0, 0, 0, pl.ds(tok, n_seg, stride=CMP_STRIDE), :].astype(BF16)
        pre = pre + _dot(x_tok, w1[tok * HEAD_DIM:(tok + 1) * HEAD_DIM, :])
    pe_term = _dot(pe_ref[...], w1)
    bias = pe_term[0:1, :hid] + pe_term[8:9, hid:]
    nxt = pltpu.roll(pre[:, hid:], n_seg - 1, 0)
    mid = jax.nn.gelu(pre[:, :hid] + nxt + bias)
    out = _dot(mid.astype(BF16), w2_ref[...])
    if rope:
        out = _rope(out, cos_ref[0], sin_ref[0])
    o_ref[0, 0] = out[:, :HEAD_DIM].astype(BF16)


def _compress(kvc, which, pe, w1, w2, cos_c, sin_c, rope):
    bsz, _, grp, s, _ = kvc.shape
    n_seg = s // CMP_STRIDE
    hid = w1.shape[1]
    half = w1.shape[0] // 2
    w1cat = jnp.concatenate([w1[:half], w1[half:]], axis=1).astype(BF16)
    w2p = jnp.pad(w2, ((0, 0), (0, LANES - HEAD_DIM))).astype(BF16)
    pe_flat = pe.reshape(2, half)
    pe_rows = jnp.zeros((16, half), F32).at[0].set(pe_flat[0]).at[8].set(pe_flat[1]).astype(BF16)
    return pl.pallas_call(
        functools.partial(_compress_kernel, rope=rope),
        out_shape=jax.ShapeDtypeStruct((bsz, grp, n_seg, HEAD_DIM), BF16),
        grid=(bsz, grp),
        in_specs=[pl.BlockSpec((1, 1, 1, s, HEAD_DIM), lambda b, g: (b, which, g, 0, 0)),
                  pl.BlockSpec((16, half), lambda b, g: (0, 0)),
                  pl.BlockSpec((half, 2 * hid), lambda b, g: (0, 0)),
                  pl.BlockSpec((hid, LANES), lambda b, g: (0, 0)),
                  pl.BlockSpec((1, n_seg, LANES), lambda b, g: (b, 0, 0)),
                  pl.BlockSpec((1, n_seg, LANES), lambda b, g: (b, 0, 0))],
        out_specs=pl.BlockSpec((1, 1, n_seg, HEAD_DIM), lambda b, g: (b, g, 0, 0)),
        compiler_params=_cparams(2),
        name="nsa_compress",
    )(kvc, pe_rows, w1cat, w2p, cos_c, sin_c)


def _masked_scores(q4, k, madd, r_heads):
    rows, tk = q4.shape[0], k.shape[0]
    s = _dot_nt(q4, k).reshape(r_heads, rows // r_heads, tk) + madd[None]
    return s.reshape(rows, tk)


def _online_softmax(scores, values, states):
    stats = []
    for s, (m, _) in zip(scores, states):
        m_new = jnp.maximum(m, jnp.max(s, axis=-1, keepdims=True))
        stats.append((m_new, jnp.exp(m - m_new), jnp.exp(s - m_new).astype(BF16)))
    pv = [_dot(st[2], v) for st, v in zip(stats, values)]
    return tuple((st[0], st[1] * state[1] + o) for st, state, o in zip(stats, states, pv))


def _softmax_result(state):
    acc = state[1]
    return acc[:, :HEAD_DIM] / acc[:, HEAD_DIM:HEAD_DIM + 1]


def _nsa_attn_kernel(q_ref, kc_ref, vc_ref, ks_ref, vs_ref, kw_ref, vw_ref, gate_ref,
                     ovt_ref, eye_ref, exp_ref, o_ref, *, tq, tk, wk, n_cmp, n_sel, ng):
    qi = pl.program_id(2)
    r_heads = NSA_Q_PER_GROUP
    rows = r_heads * tq
    gw = r_heads * HEAD_DIM
    t0 = qi * tq
    n_cp = kc_ref.shape[2]
    n_slc = ovt_ref.shape[0]
    groups = range(ng)

    q4 = []
    for g in groups:
        qf = q_ref[0, :, g * gw:(g + 1) * gw].astype(F32)
        q4.append(jnp.concatenate([qf[:, r * HEAD_DIM:(r + 1) * HEAD_DIM] for r in range(r_heads)],
                                  axis=0).astype(BF16))

    t_c = t0 + lax.broadcasted_iota(jnp.int32, (tq, n_cp), 0)
    n_c = lax.broadcasted_iota(jnp.int32, (tq, n_cp), 1)
    cmp_ok = (n_c * CMP_STRIDE + (CMP_LEN - 1) <= t_c) & (n_c < n_cmp)
    any_ok = (t_c[:, 0:1] >= CMP_LEN - 1).astype(F32)
    sc = [_dot_nt(q4[g], kc_ref[0, g]).reshape(r_heads, tq, n_cp) for g in groups]
    p_cmp = []
    for g in groups:
        s = jnp.where(cmp_ok[None], sc[g], NEG_INF)
        e = jnp.exp(s - jnp.max(s, axis=-1, keepdims=True))
        p_cmp.append(e * (any_ok[None] / jnp.sum(e, axis=-1, keepdims=True)))
    o_cmp = [_dot(p_cmp[g].reshape(rows, n_cp).astype(BF16), vc_ref[0, g]) for g in groups]

    ovt = ovt_ref[...]
    imp_t = []
    for g in groups:
        p_hi, p_lo = _split_bf16(jnp.sum(p_cmp[g], axis=0))
        imp_t.append(_dot_nt(ovt, p_hi) + _dot_nt(ovt, p_lo))
    blk = lax.broadcasted_iota(jnp.int32, (n_slc, tq), 0)
    tok = t0 + lax.broadcasted_iota(jnp.int32, (n_slc, tq), 1)
    cur = lax.shift_right_logical(tok, SLC_LEN.bit_length() - 1)
    forced = (blk == 0) | (blk == cur) | (blk == cur - 1)
    causal_blk = blk * SLC_LEN <= tok
    neg_t = []
    for g in groups:
        score = jnp.where(causal_blk, jnp.where(forced, FORCE_BONUS, imp_t[g]), NEG_INF)
        rank = jnp.zeros((n_slc, tq), jnp.int32)
        for i in range(n_slc):
            s_i = score[i:i + 1, :]
            ahead = (s_i > score) | ((s_i == score) & (blk > i))
            rank = rank + jnp.where(ahead, 1, 0)
        neg_t.append(jnp.where(rank < n_sel, 0.0, NEG_INF).astype(BF16))
    eye = eye_ref[...]
    neg = [_dot_nt(eye, neg_t[g]).astype(BF16) for g in groups]

    t_k = t0 + lax.broadcasted_iota(jnp.int32, (tq, tk), 0)
    c_k = lax.broadcasted_iota(jnp.int32, (tq, tk), 1)

    def slc_scores(kj):
        start = pl.multiple_of(kj * tk, tk)
        expand = exp_ref[kj]
        causal = c_k + kj * tk <= t_k
        madd = [jnp.where(causal, _dot(neg[g], expand), NEG_INF) for g in groups]
        return [_masked_scores(q4[g], ks_ref[0, g, pl.ds(start, tk), :], madd[g], r_heads) for g in groups]

    def slc_values(kj):
        start = pl.multiple_of(kj * tk, tk)
        return [vs_ref[0, g, pl.ds(start, tk), :] for g in groups]

    last = (t0 + tq - 1) // tk
    init = (jnp.full((rows, 1), NEG_INF, F32), jnp.zeros((rows, LANES), F32))
    states = lax.fori_loop(0, last, lambda kj, st: _online_softmax(slc_scores(kj), slc_values(kj), st),
                           (init,) * ng)

    start_w = pl.multiple_of(jnp.maximum(t0 + tq - wk, 0), tq)
    t_w = t0 + lax.broadcasted_iota(jnp.int32, (tq, wk), 0)
    key_w = start_w + lax.broadcasted_iota(jnp.int32, (tq, wk), 1)
    madd_w = jnp.where((key_w <= t_w) & (key_w > t_w - WINDOW), 0.0, NEG_INF)
    s_win = [_masked_scores(q4[g], kw_ref[0, g, pl.ds(start_w, wk), :], madd_w, r_heads) for g in groups]
    v_win = [vw_ref[0, g, pl.ds(start_w, wk), :] for g in groups]
    done = _online_softmax(slc_scores(last) + s_win, slc_values(last) + v_win, states + (init,) * ng)

    for g in groups:
        o_slc = _softmax_result(done[g])
        o_win = _softmax_result(done[ng + g])
        gates = gate_ref[0, :, g * LANES:(g + 1) * LANES]
        pieces = []
        for r in range(r_heads):
            sl = slice(r * tq, (r + 1) * tq)
            pieces.append(gates[:, r:r + 1] * o_cmp[g][sl]
                          + gates[:, r_heads + r:r_heads + r + 1] * o_slc[sl]
                          + gates[:, 2 * r_heads + r:2 * r_heads + r + 1] * o_win[sl])
        o_ref[0, :, g * gw:(g + 1) * gw] = jnp.concatenate(pieces, axis=1).astype(BF16)


def _nsa_attention(q, kk, vv, k_cmp, v_cmp, gates, tq=128, tk=512, ng=4):
    bsz, s, d = q.shape
    grp = NSA_KV_GROUPS
    gw = NSA_Q_PER_GROUP * HEAD_DIM
    n_cp = k_cmp.shape[2]
    n_cmp = (s - CMP_LEN) // CMP_STRIDE + 1
    n_slc = s // SLC_LEN
    n_sel = min(N_SELECT, n_slc)
    tq = min(tq, s)
    tk = min(tk, s)
    c0 = np.arange(n_cp)[None, :] * CMP_STRIDE
    s0 = np.arange(n_slc)[:, None] * SLC_LEN
    ov = np.maximum(np.minimum(c0 + CMP_LEN, s0 + SLC_LEN) - np.maximum(c0, s0), 0) / CMP_LEN
    ov = ov * (np.arange(n_cp)[None, :] < n_cmp)
    ovt = jnp.asarray(ov, BF16)
    eye = jnp.asarray(np.eye(tq, dtype=np.float32), BF16)
    n_kt = s // tk
    expand = (np.arange(n_slc)[None, :, None]
              == (np.arange(n_kt)[:, None, None] * (tk // SLC_LEN) + np.arange(tk)[None, None, :] // SLC_LEN))
    expand = jnp.asarray(expand.astype(np.float32), BF16)

    wk = min(-(-(WINDOW + tq - 1) // tq) * tq, s)

    def kv_spec(i, width):
        return pl.BlockSpec((1, 1, ng, s, width), lambda b, g, t, i=i: (b, i, g, 0, 0),
                            pipeline_mode=pl.Buffered(1))

    def kernel(q_ref, kc_ref, vc_ref, ks_ref, vs_ref, kw_ref, vw_ref, *rest):
        return _nsa_attn_kernel(q_ref, kc_ref, vc_ref, ks_ref.at[0], vs_ref.at[0], kw_ref.at[0],
                                vw_ref.at[0], *rest, tq=tq, tk=tk, wk=wk, n_cmp=n_cmp, n_sel=n_sel, ng=ng)

    return pl.pallas_call(
        kernel,
        out_shape=jax.ShapeDtypeStruct((bsz, s, d), BF16),
        grid=(bsz, grp // ng, s // tq),
        in_specs=[pl.BlockSpec((1, tq, ng * gw), lambda b, g, t: (b, t, g)),
                  pl.BlockSpec((1, ng, n_cp, HEAD_DIM), lambda b, g, t: (b, g, 0, 0)),
                  pl.BlockSpec((1, ng, n_cp, HEAD_DIM), lambda b, g, t: (b, g, 0, 0)),
                  kv_spec(0, HEAD_DIM), kv_spec(0, LANES), kv_spec(1, HEAD_DIM), kv_spec(1, LANES),
                  pl.BlockSpec((1, tq, ng * LANES), lambda b, g, t: (b, t, g)),
                  pl.BlockSpec((n_slc, n_cp), lambda b, g, t: (0, 0)),
                  pl.BlockSpec((tq, tq), lambda b, g, t: (0, 0)),
                  pl.BlockSpec((n_kt, n_slc, tk), lambda b, g, t: (0, 0, 0))],
        out_specs=pl.BlockSpec((1, tq, ng * gw), lambda b, g, t: (b, t, g)),
        compiler_params=_cparams(3),
        name="nsa_attention",
    )(q, k_cmp, v_cmp, kk, vv, kk, vv, gates, ovt, eye, expand)


_HALO = 32


def _dwconv_kernel(x_ref, prev_ref, dw_ref, dwb_ref, lng_ref, lnb_ref, o_ref, ext_ref, *, tm):
    i = pl.program_id(1)
    prev = prev_ref[0]
    ext_ref[0:_HALO, :] = jnp.where(i > 0, prev, jnp.zeros_like(prev))
    ext_ref[_HALO:_HALO + tm, :] = x_ref[0]
    off = _HALO - (CONV_WIDTH - 1)
    acc = jnp.zeros(x_ref.shape[1:], F32) + dwb_ref[...]
    for k in range(CONV_WIDTH):
        acc = acc + ext_ref[off + k:off + k + tm, :] * dw_ref[k:k + 1, :]
    mu = jnp.mean(acc, axis=-1, keepdims=True)
    cen = acc - mu
    var = jnp.mean(cen * cen, axis=-1, keepdims=True)
    y = cen * lax.rsqrt(var + NORM_EPS) * lng_ref[...] + lnb_ref[...]
    o_ref[0] = (y * jax.nn.sigmoid(y)).astype(BF16)


def _dwconv_ln_swish(x, dw, dw_b, ln_g, ln_b, tm=256):
    bsz, s, d = x.shape
    tm = min(tm, s)
    per = tm // _HALO
    return pl.pallas_call(
        functools.partial(_dwconv_kernel, tm=tm),
        out_shape=jax.ShapeDtypeStruct((bsz, s, d), BF16),
        grid=(bsz, s // tm),
        in_specs=[pl.BlockSpec((1, tm, d), lambda b, i: (b, i, 0)),
                  pl.BlockSpec((1, _HALO, d), lambda b, i: (b, jnp.maximum(i * per - 1, 0), 0)),
                  pl.BlockSpec((_HALO, d), lambda b, i: (0, 0)),
                  pl.BlockSpec((1, d), lambda b, i: (0, 0)),
                  pl.BlockSpec((1, d), lambda b, i: (0, 0)),
                  pl.BlockSpec((1, d), lambda b, i: (0, 0))],
        out_specs=pl.BlockSpec((1, tm, d), lambda b, i: (b, i, 0)),
        scratch_shapes=[pltpu.VMEM((_HALO + tm, d), F32)],
        compiler_params=_cparams(2),
        name="dwconv_ln_swish",
    )(x, x, dw, dw_b, ln_g, ln_b)


def _rope_tables(positions):
    half = ROPE_DIM // 2
    inv_freq = ROPE_THETA ** (-jnp.arange(half, dtype=F32) / half)
    ang = positions.astype(F32)[..., None] * inv_freq
    cos, sin = jnp.cos(ang), jnp.sin(ang)
    ones = jnp.ones(ang.shape[:-1] + (HEAD_DIM - ROPE_DIM,), F32)
    cos_h = jnp.concatenate([cos, cos, ones], axis=-1)
    sin_h = jnp.concatenate([-sin, sin, 0.0 * ones], axis=-1)
    reps = LANES // HEAD_DIM
    return jnp.concatenate([cos_h] * reps, axis=-1), jnp.concatenate([sin_h] * reps, axis=-1)


def _nsa_weight(w_in, d):
    kv_end = d + 6 * NSA_KV_GROUPS * HEAD_DIM
    wg = w_in[:, kv_end:].reshape(d, NSA_KV_GROUPS, NSA_Q_PER_GROUP, 3)
    wg = wg.transpose(0, 1, 3, 2).reshape(d, NSA_KV_GROUPS, 3 * NSA_Q_PER_GROUP)
    wg = jnp.pad(wg, ((0, 0), (0, 0), (0, LANES - 3 * NSA_Q_PER_GROUP)))
    return jnp.concatenate([w_in[:, :kv_end], wg.reshape(d, NSA_KV_GROUPS * LANES)], axis=1).astype(BF16)


def kernel(x, c, positions, ada_w, ada_b, mix_pre_g, mix_post_g, ffn_pre_g, ffn_post_g, ffn_w1, ffn_w2, sb_w_in, sb_w_out, nsa_w_in, nsa_w_out, nsa_pe_k, nsa_w1_k, nsa_w2_k, nsa_pe_v, nsa_w1_v, nsa_w2_v, cv_w_in, cv_b_in, cv_dw, cv_dw_b, cv_ln_g, cv_ln_b, cv_w_out, cv_b_out):
    bsz, s, d = x.shape
    depth = ada_w.shape[0]
    n_mixers = 3
    mod = _ada_mod(c, ada_w, ada_b).reshape(depth, bsz, 6, 1, d)
    zero_bias = jnp.zeros((1, d), F32)
    h = x
    for i in range(depth):
        sh1, sc1, g1, sh2, sc2, g2 = [mod[i, :, m] for m in range(6)]
        pre_g = mix_pre_g[i].reshape(1, d)
        post_g = mix_post_g[i].reshape(1, d)
        kind, j = i % n_mixers, i // n_mixers
        if kind == 0:
            qkv = _sb_proj(h, pre_g, sc1, sh1, sb_w_in[j].astype(BF16))
            o = _sb_attention(qkv)
            h = _out_proj(o, sb_w_out[j].astype(BF16), zero_bias, h, g1, post_g)
        elif kind == 1:
            cos_t, sin_t = _rope_tables(positions)
            q, kvc, kk, vv, gates = _nsa_proj(h, pre_g, sc1, sh1, _nsa_weight(nsa_w_in[j], d), cos_t, sin_t)
            end = jnp.minimum(jnp.arange(s // CMP_STRIDE) * CMP_STRIDE + CMP_LEN - 1, s - 1)
            cos_c, sin_c = _rope_tables(positions[:, end])
            k_cmp = _compress(kvc, 0, nsa_pe_k[j], nsa_w1_k[j], nsa_w2_k[j], cos_c, sin_c, True)
            v_cmp = _compress(kvc, 1, nsa_pe_v[j], nsa_w1_v[j], nsa_w2_v[j], cos_c, sin_c, False)
            o = _nsa_attention(q, kk, vv, k_cmp, v_cmp, gates)
            h = _out_proj(o, nsa_w_out[j].astype(BF16), zero_bias, h, g1, post_g)
        else:
            a = _cv_proj(h, pre_g, sc1, sh1, cv_w_in[j].astype(BF16), cv_b_in[j].reshape(1, 2 * d))
            dw = jnp.pad(cv_dw[j].reshape(CONV_WIDTH, d), ((0, _HALO - CONV_WIDTH), (0, 0)))
            a = _dwconv_ln_swish(a, dw, cv_dw_b[j].reshape(1, d), cv_ln_g[j].reshape(1, d),
                                 cv_ln_b[j].reshape(1, d))
            h = _out_proj(a, cv_w_out[j].astype(BF16), cv_b_out[j].reshape(1, d), h, g1, post_g)
        h = _mlp(h, ffn_pre_g[i].reshape(1, d), sc2, sh2, ffn_w1[i].astype(BF16),
                 ffn_w2[i].astype(BF16), g2, ffn_post_g[i].reshape(1, d))
    return h
```

```python
import functools

import numpy as np
import jax
import jax.numpy as jnp
from jax import lax
from jax.experimental import pallas as pl
from jax.experimental.pallas import tpu as pltpu

F32 = jnp.float32
BF16 = jnp.bfloat16

N_HEADS = 16
HEAD_DIM = 64
ROPE_THETA = 500000.0
ROPE_DIM = HEAD_DIM // 4
NORM_EPS = 1e-6
NSA_KV_GROUPS = 4
NSA_Q_PER_GROUP = N_HEADS // NSA_KV_GROUPS
CMP_LEN = 32
CMP_STRIDE = 16
SLC_LEN = 64
N_SELECT = 16
WINDOW = 512
FORCE_BONUS = 1e4
NEG_INF = -1e30
CONV_WIDTH = 31
ATTN_SCALE = HEAD_DIM ** -0.5
LOG2_E = 1.4426950408889634

LANES = 128
_SUBLANES = 8
VMEM_LIMIT_BYTES = 56 * 1024 * 1024

_NT = (((1,), (1,)), ((), ()))


def _cparams(n_axes):
    return pltpu.CompilerParams(dimension_semantics=("arbitrary",) * n_axes,
                                vmem_limit_bytes=VMEM_LIMIT_BYTES)


def _dot(a, b):
    return jnp.dot(a, b, preferred_element_type=F32)


def _dot_nt(a, b):
    return lax.dot_general(a, b, _NT, preferred_element_type=F32)


def _split_bf16(x):
    hi = x.astype(BF16)
    lo = (x - hi.astype(F32)).astype(BF16)
    return hi, lo


def _rms(x):
    return x * lax.rsqrt(jnp.mean(x * x, axis=-1, keepdims=True) + NORM_EPS)


def _norm_mod(h, g, sc, sh):
    return (_rms(h) * g) * (1.0 + sc) + sh


def _rope(x, cos_t, sin_t):
    rows, w = x.shape
    reps = w // LANES
    cos_w = jnp.concatenate([cos_t] * reps, axis=1) if reps > 1 else cos_t
    sin_w = jnp.concatenate([sin_t] * reps, axis=1) if reps > 1 else sin_t
    half = ROPE_DIM // 2
    lane = lax.broadcasted_iota(jnp.int32, (rows, w), 1)
    first_half = (lane & (HEAD_DIM - 1)) < half
    partner = jnp.where(first_half, pltpu.roll(x, w - half, 1), pltpu.roll(x, half, 1))
    return x * cos_w + partner * sin_w


def _ada_kernel(c_ref, w_ref, b_ref, o_ref):
    c = c_ref[...]
    cond = c * jax.nn.sigmoid(c)
    o_ref[0] = jnp.dot(cond, w_ref[0], preferred_element_type=F32,
                       precision=lax.Precision.HIGHEST) + b_ref[0]


def _ada_mod(c, ada_w, ada_b):
    depth, d, n = ada_w.shape
    b = c.shape[0]
    tn = 1024
    return pl.pallas_call(
        _ada_kernel,
        out_shape=jax.ShapeDtypeStruct((depth, b, n), F32),
        grid=(depth, n // tn),
        in_specs=[pl.BlockSpec((b, d), lambda i, j: (0, 0)),
                  pl.BlockSpec((1, d, tn), lambda i, j: (i, 0, j)),
                  pl.BlockSpec((1, 1, tn), lambda i, j: (i, 0, j))],
        out_specs=pl.BlockSpec((1, b, tn), lambda i, j: (i, 0, j)),
        compiler_params=_cparams(2),
        name="ada_mod",
    )(c, ada_w, ada_b.reshape(depth, 1, n))


def _proj_in_specs(tm, d, n_w):
    return [pl.BlockSpec((1, tm, d), lambda b, i: (b, i, 0)),
            pl.BlockSpec((1, d), lambda b, i: (0, 0)),
            pl.BlockSpec((1, 1, d), lambda b, i: (b, 0, 0)),
            pl.BlockSpec((1, 1, d), lambda b, i: (b, 0, 0)),
            pl.BlockSpec((d, n_w), lambda b, i: (0, 0))]


def _sb_proj_kernel(h_ref, g_ref, sc_ref, sh_ref, w_ref, o_ref, *, tn):
    d = h_ref.shape[2]
    u = _norm_mod(h_ref[0], g_ref[...], sc_ref[0], sh_ref[0]).astype(BF16)
    for j in range(w_ref.shape[1] // tn):
        acc = _dot(u, w_ref[:, j * tn:(j + 1) * tn])
        if j * tn < d:
            acc = acc * (-ATTN_SCALE)
        o_ref[0, :, j * tn:(j + 1) * tn] = acc.astype(BF16)


def _sb_proj(h, g, sc, sh, w, tm=512, tn=512):
    bsz, s, d = h.shape
    n = w.shape[1]
    return pl.pallas_call(
        functools.partial(_sb_proj_kernel, tn=tn),
        out_shape=jax.ShapeDtypeStruct((bsz, s, n), BF16),
        grid=(bsz, s // tm),
        in_specs=_proj_in_specs(tm, d, n),
        out_specs=pl.BlockSpec((1, tm, n), lambda b, i: (b, i, 0)),
        compiler_params=_cparams(2),
        name="sb_proj",
    )(h, g, sc, sh, w)


def _nsa_proj_kernel(h_ref, g_ref, sc_ref, sh_ref, w_ref, cos_ref, sin_ref,
                     q_ref, kvc_ref, kk_ref, vv_ref, gate_ref, *, tn):
    d = h_ref.shape[2]
    tm = h_ref.shape[1]
    kv_w = NSA_KV_GROUPS * HEAD_DIM
    u = _norm_mod(h_ref[0], g_ref[...], sc_ref[0], sh_ref[0]).astype(BF16)
    cos_t = cos_ref[0]
    sin_t = sin_ref[0]
    one_col = jnp.where(lax.broadcasted_iota(jnp.int32, (tm, LANES - HEAD_DIM), 1) == 0, 1.0, 0.0)
    n_q = d // tn
    n_kv = 6 * kv_w // tn
    for j in range(w_ref.shape[1] // tn):
        acc = _dot(u, w_ref[:, j * tn:(j + 1) * tn])
        if j < n_q:
            q = _rope(acc, cos_t, sin_t) * ATTN_SCALE
            q_ref[0, :, j * tn:(j + 1) * tn] = q.astype(BF16)
        elif j < n_q + n_kv:
            for part in range(tn // kv_w):
                i = (j - n_q) * (tn // kv_w) + part
                x = acc[:, part * kv_w:(part + 1) * kv_w]
                if i in (2, 4):
                    x = _rope(x, cos_t, sin_t)
                for grp in range(NSA_KV_GROUPS):
                    xg = x[:, grp * HEAD_DIM:(grp + 1) * HEAD_DIM]
                    if i < 2:
                        kvc_ref[0, i, grp] = xg
                    elif i in (2, 4):
                        kk_ref[0, (i - 2) // 2, grp] = xg.astype(BF16)
                    else:
                        vv_ref[0, (i - 3) // 2, grp] = jnp.concatenate([xg, one_col], axis=1).astype(BF16)
        else:
            c0 = (j - n_q - n_kv) * tn
            gate_ref[0, :, c0:c0 + tn] = jax.nn.sigmoid(acc)


def _nsa_proj(h, g, sc, sh, w, cos_t, sin_t, tm=512, tn=256):
    bsz, s, d = h.shape
    n = w.shape[1]
    n_gate = NSA_KV_GROUPS * LANES
    return pl.pallas_call(
        functools.partial(_nsa_proj_kernel, tn=tn),
        out_shape=(jax.ShapeDtypeStruct((bsz, s, d), BF16),
                   jax.ShapeDtypeStruct((bsz, 2, NSA_KV_GROUPS, s, HEAD_DIM), F32),
                   jax.ShapeDtypeStruct((bsz, 2, NSA_KV_GROUPS, s, HEAD_DIM), BF16),
                   jax.ShapeDtypeStruct((bsz, 2, NSA_KV_GROUPS, s, LANES), BF16),
                   jax.ShapeDtypeStruct((bsz, s, n_gate), F32)),
        grid=(bsz, s // tm),
        in_specs=_proj_in_specs(tm, d, n) + [
            pl.BlockSpec((1, tm, LANES), lambda b, i: (b, i, 0)),
            pl.BlockSpec((1, tm, LANES), lambda b, i: (b, i, 0))],
        out_specs=(pl.BlockSpec((1, tm, d), lambda b, i: (b, i, 0)),
                   pl.BlockSpec((1, 2, NSA_KV_GROUPS, tm, HEAD_DIM), lambda b, i: (b, 0, 0, i, 0)),
                   pl.BlockSpec((1, 2, NSA_KV_GROUPS, tm, HEAD_DIM), lambda b, i: (b, 0, 0, i, 0)),
                   pl.BlockSpec((1, 2, NSA_KV_GROUPS, tm, LANES), lambda b, i: (b, 0, 0, i, 0)),
                   pl.BlockSpec((1, tm, n_gate), lambda b, i: (b, i, 0))),
        compiler_params=_cparams(2),
        name="nsa_proj",
    )(h, g, sc, sh, w, cos_t, sin_t)


def _cv_proj_kernel(h_ref, g_ref, sc_ref, sh_ref, w_ref, b_ref, o_ref, *, tn):
    d = h_ref.shape[2]
    u = _norm_mod(h_ref[0], g_ref[...], sc_ref[0], sh_ref[0]).astype(BF16)
    for j in range(d // tn):
        a = _dot(u, w_ref[:, j * tn:(j + 1) * tn]) + b_ref[:, j * tn:(j + 1) * tn]
        gt = _dot(u, w_ref[:, d + j * tn:d + (j + 1) * tn]) + b_ref[:, d + j * tn:d + (j + 1) * tn]
        o_ref[0, :, j * tn:(j + 1) * tn] = a * jax.nn.sigmoid(gt)


def _cv_proj(h, g, sc, sh, w, bias, tm=512, tn=256):
    bsz, s, d = h.shape
    n = w.shape[1]
    return pl.pallas_call(
        functools.partial(_cv_proj_kernel, tn=tn),
        out_shape=jax.ShapeDtypeStruct((bsz, s, d), F32),
        grid=(bsz, s // tm),
        in_specs=_proj_in_specs(tm, d, n) + [pl.BlockSpec((1, n), lambda b, i: (0, 0))],
        out_specs=pl.BlockSpec((1, tm, d), lambda b, i: (b, i, 0)),
        compiler_params=_cparams(2),
        name="cv_proj",
    )(h, g, sc, sh, w, bias)


def _out_mlp_kernel(a_ref, wo_ref, bo_ref, h_ref, g1_ref, pg1_ref, pre_ref, sc_ref, sh_ref,
                    w1_ref, w2_ref, g2_ref, pg2_ref, o_ref, h1_ref, hid_ref, *, tf, tn):
    a = a_ref[0]
    d = wo_ref.shape[1]
    d_ff = w1_ref.shape[1]
    for j in range(d // tn):
        o_ref[0, :, j * tn:(j + 1) * tn] = (_dot(a, wo_ref[:, j * tn:(j + 1) * tn])
                                            + bo_ref[:, j * tn:(j + 1) * tn])
    h1_ref[...] = h_ref[0] + g1_ref[0] * (_rms(o_ref[0]) * pg1_ref[...])
    u = _norm_mod(h1_ref[...], pre_ref[...], sc_ref[0], sh_ref[0]).astype(BF16)
    for c in range(d_ff // tf):
        act = jnp.maximum(_dot(u, w1_ref[:, c * tf:(c + 1) * tf]), 0.0)
        hid_ref[:, c * tf:(c + 1) * tf] = (act * act).astype(BF16)
    for j in range(d // tn):
        o_ref[0, :, j * tn:(j + 1) * tn] = _dot(hid_ref[...], w2_ref[:, j * tn:(j + 1) * tn])
    o_ref[0] = h1_ref[...] + g2_ref[0] * (_rms(o_ref[0]) * pg2_ref[...])


def _out_mlp(a, w_out, b_out, h, g1, post_g1, pre_g2, sc2, sh2, w1, w2, g2, post_g2,
             tm=512, tf=512, tn=256):
    bsz, s, k = a.shape
    d = w_out.shape[1]
    d_ff = w1.shape[1]

    def rows(width):
        return pl.BlockSpec((1, tm, width), lambda b, i: (b, i, 0))

    def per_batch():
        return pl.BlockSpec((1, 1, d), lambda b, i: (b, 0, 0))

    def whole(shape, single=False):
        return pl.BlockSpec(shape, lambda b, i: (0, 0), pipeline_mode=pl.Buffered(1) if single else None)

    return pl.pallas_call(
        functools.partial(_out_mlp_kernel, tf=tf, tn=tn),
        out_shape=jax.ShapeDtypeStruct((bsz, s, d), F32),
        grid=(bsz, s // tm),
        in_specs=[rows(k), whole((k, d), True), whole((1, d)), rows(d), per_batch(), whole((1, d)),
                  whole((1, d)), per_batch(), per_batch(),
                  whole((d, d_ff), True), whole((d_ff, d), True), per_batch(), whole((1, d))],
        out_specs=rows(d),
        scratch_shapes=[pltpu.VMEM((tm, d), F32), pltpu.VMEM((tm, d_ff), BF16)],
        compiler_params=_cparams(2),
        name="out_mlp",
    )(a, w_out, b_out, h, g1, post_g1, pre_g2, sc2, sh2, w1, w2, g2, post_g2)


SB_ZERO_WEIGHT_LOG = -110.0


def _sb_tiles(qs, ks, vs, tri2, state, strict):
    n = len(qs)
    ws = [_dot_nt(qs[h], ks[h]) for h in range(n)]
    lks, hls = [], []
    for w in ws:
        lk = jnp.minimum(w, 0.0) - jnp.log(1.0 + jnp.exp2(jnp.abs(w) * (-LOG2_E)))
        if strict is not None:
            lk = jnp.where(strict, lk, 0.0)
        hi, lo = _split_bf16(lk)
        lks.append(lk)
        hls.append(jnp.concatenate([hi, lo], axis=1))
    tails = [_dot(hl, tri2) for hl in hls]
    probs = []
    for h in range(n):
        a = jnp.exp((tails[h] - ws[h]) + state[2 * h])
        if strict is not None:
            a = jnp.where(strict, a, 0.0)
        probs.append(a.astype(BF16))
    out = ()
    for h in range(n):
        out += (state[2 * h] + jnp.sum(lks[h], axis=-1, keepdims=True),
                state[2 * h + 1] + _dot(probs[h], vs[h]))
    return out


def _sb_attn_kernel(q_ref, k_ref, v_ref, tri_ref, o_ref, *, t, pairs):
    qi = pl.program_id(2)
    tri2 = tri_ref[...]
    lane = lax.broadcasted_iota(jnp.int32, (t, LANES), 1)
    row = lax.broadcasted_iota(jnp.int32, (t, t), 0)
    col = lax.broadcasted_iota(jnp.int32, (t, t), 1)
    strict = col < row
    q_heads = []
    for p in range(pairs):
        q = q_ref[0, :, p * LANES:(p + 1) * LANES]
        zero_q = jnp.zeros_like(q)
        q_heads += [jnp.where(lane < HEAD_DIM, q, zero_q),
                    jnp.where(lane >= HEAD_DIM, q, zero_q)]
    n_heads = 2 * pairs

    def tiles(kj, state, mask):
        start = pl.multiple_of(kj * t, t)
        cols = [slice((hd // 2) * LANES, (hd // 2 + 1) * LANES) for hd in range(n_heads)]
        ks = [k_ref[0, pl.ds(start, t), c] for c in cols]
        vs = [v_ref[0, pl.ds(start, t), c] for c in cols]
        return _sb_tiles(q_heads, ks, vs, tri2, state, mask)

    def largest(state):
        carries = state[0::2]
        top = carries[0]
        for c in carries[1:]:
            top = jnp.maximum(top, c)
        return jnp.max(top)

    zeros = (jnp.zeros((t, 1), F32), jnp.zeros((t, LANES), F32))
    state = tiles(qi, zeros * n_heads, strict)

    def cond(loop):
        return (loop[0] < qi) & (loop[1] > SB_ZERO_WEIGHT_LOG)

    def body(loop):
        state = tiles(qi - 1 - loop[0], loop[2:], None)
        return (loop[0] + 1, largest(state)) + state

    out = lax.while_loop(cond, body, (jnp.int32(0), largest(state)) + state)[2:]
    for p in range(pairs):
        o_ref[0, :, p * LANES:(p + 1) * LANES] = jnp.where(
            lane < HEAD_DIM, out[4 * p + 1], out[4 * p + 3]).astype(BF16)


def _sb_attention(qkv, t=256, pairs=2):
    bsz, s, n3 = qkv.shape
    d = n3 // 3
    w = pairs * LANES
    n_grp = d // w
    t = min(t, s)
    tri = np.tril(np.ones((t, t), np.float32))
    tri = jnp.asarray(np.concatenate([tri, tri], axis=0), BF16)
    return pl.pallas_call(
        functools.partial(_sb_attn_kernel, t=t, pairs=pairs),
        out_shape=jax.ShapeDtypeStruct((bsz, s, d), BF16),
        grid=(bsz, n_grp, s // t),
        in_specs=[pl.BlockSpec((1, t, w), lambda b, p, i: (b, i, p)),
                  pl.BlockSpec((1, s, w), lambda b, p, i: (b, 0, n_grp + p)),
                  pl.BlockSpec((1, s, w), lambda b, p, i: (b, 0, 2 * n_grp + p)),
                  pl.BlockSpec((2 * t, t), lambda b, p, i: (0, 0))],
        out_specs=pl.BlockSpec((1, t, w), lambda b, p, i: (b, i, p)),
        compiler_params=_cparams(3),
        name="sb_attention",
    )(qkv, qkv, qkv, tri)


def _compress_kernel(x_ref, pe_ref, w1_ref, w2_ref, cos_ref, sin_ref, o_ref, *, rope):
    w1 = w1_ref[...]
    hid = w1.shape[1] // 2
    n_seg = o_ref.shape[2]
    pre = jnp.zeros((n_seg, 2 * hid), F32)
    for tok in range(CMP_STRIDE):
        x_tok = x_ref[0, 0, 0, pl.ds(tok, n_seg, stride=CMP_STRIDE), :].astype(BF16)
        pre = pre + _dot(x_tok, w1[tok * HEAD_DIM:(tok + 1) * HEAD_DIM, :])
    pe_term = _dot(pe_ref[...], w1)
    bias = pe_term[0:1, :hid] + pe_term[8:9, hid:]
    nxt = pltpu.roll(pre[:, hid:], n_seg - 1, 0)
    mid = jax.nn.gelu(pre[:, :hid] + nxt + bias)
    out = _dot(mid.astype(BF16), w2_ref[...])
    if rope:
        out = _rope(out, cos_ref[0], sin_ref[0])
    o_ref[0, 0] = out[:, :HEAD_DIM].astype(BF16)


def _compress(kvc, which, pe, w1, w2, cos_c, sin_c, rope):
    bsz, _, grp, s, _ = kvc.shape
    n_seg = s // CMP_STRIDE
    hid = w1.shape[1]
    half = w1.shape[0] // 2
    w1cat = jnp.concatenate([w1[:half], w1[half:]], axis=1).astype(BF16)
    w2p = jnp.pad(w2, ((0, 0), (0, LANES - HEAD_DIM))).astype(BF16)
    pe_flat = pe.reshape(2, half)
    pe_rows = jnp.zeros((16, half), F32).at[0].set(pe_flat[0]).at[8].set(pe_flat[1]).astype(BF16)
    return pl.pallas_call(
        functools.partial(_compress_kernel, rope=rope),
        out_shape=jax.ShapeDtypeStruct((bsz, grp, n_seg, HEAD_DIM), BF16),
        grid=(bsz, grp),
        in_specs=[pl.BlockSpec((1, 1, 1, s, HEAD_DIM), lambda b, g: (b, which, g, 0, 0)),
                  pl.BlockSpec((16, half), lambda b, g: (0, 0)),
                  pl.BlockSpec((half, 2 * hid), lambda b, g: (0, 0)),
                  pl.BlockSpec((hid, LANES), lambda b, g: (0, 0)),
                  pl.BlockSpec((1, n_seg, LANES), lambda b, g: (b, 0, 0)),
                  pl.BlockSpec((1, n_seg, LANES), lambda b, g: (b, 0, 0))],
        out_specs=pl.BlockSpec((1, 1, n_seg, HEAD_DIM), lambda b, g: (b, g, 0, 0)),
        compiler_params=_cparams(2),
        name="nsa_compress",
    )(kvc, pe_rows, w1cat, w2p, cos_c, sin_c)


def _masked_scores(q4, k, madd, r_heads):
    rows, tk = q4.shape[0], k.shape[0]
    s = _dot_nt(q4, k).reshape(r_heads, rows // r_heads, tk) + madd[None]
    return s.reshape(rows, tk)


def _online_softmax(scores, values, states):
    stats = []
    for s, (m, _) in zip(scores, states):
        m_new = jnp.maximum(m, jnp.max(s, axis=-1, keepdims=True))
        stats.append((m_new, jnp.exp(m - m_new), jnp.exp(s - m_new).astype(BF16)))
    pv = [_dot(st[2], v) for st, v in zip(stats, values)]
    return tuple((st[0], st[1] * state[1] + o) for st, state, o in zip(stats, states, pv))


def _softmax_result(state):
    acc = state[1]
    return acc[:, :HEAD_DIM] / acc[:, HEAD_DIM:HEAD_DIM + 1]


def _nsa_attn_kernel(q_ref, kc_ref, vc_ref, ks_ref, vs_ref, kw_ref, vw_ref, gate_ref,
                     ovt_ref, eye_ref, exp_ref, o_ref, *, tq, tk, wk, n_cmp, n_sel, ng):
    qi = pl.program_id(2)
    r_heads = NSA_Q_PER_GROUP
    rows = r_heads * tq
    gw = r_heads * HEAD_DIM
    t0 = qi * tq
    n_cp = kc_ref.shape[2]
    n_slc = ovt_ref.shape[0]
    groups = range(ng)

    q4 = []
    for g in groups:
        qf = q_ref[0, :, g * gw:(g + 1) * gw].astype(F32)
        q4.append(jnp.concatenate([qf[:, r * HEAD_DIM:(r + 1) * HEAD_DIM] for r in range(r_heads)],
                                  axis=0).astype(BF16))

    t_c = t0 + lax.broadcasted_iota(jnp.int32, (tq, n_cp), 0)
    n_c = lax.broadcasted_iota(jnp.int32, (tq, n_cp), 1)
    cmp_ok = (n_c * CMP_STRIDE + (CMP_LEN - 1) <= t_c) & (n_c < n_cmp)
    any_ok = (t_c[:, 0:1] >= CMP_LEN - 1).astype(F32)
    sc = [_dot_nt(q4[g], kc_ref[0, g]).reshape(r_heads, tq, n_cp) for g in groups]
    p_cmp = []
    for g in groups:
        s = jnp.where(cmp_ok[None], sc[g], NEG_INF)
        e = jnp.exp(s - jnp.max(s, axis=-1, keepdims=True))
        p_cmp.append(e * (any_ok[None] / jnp.sum(e, axis=-1, keepdims=True)))
    o_cmp = [_dot(p_cmp[g].reshape(rows, n_cp).astype(BF16), vc_ref[0, g]) for g in groups]

    ovt = ovt_ref[...]
    imp_t = []
    for g in groups:
        p_hi, p_lo = _split_bf16(jnp.sum(p_cmp[g], axis=0))
        imp_t.append(_dot_nt(ovt, p_hi) + _dot_nt(ovt, p_lo))
    blk = lax.broadcasted_iota(jnp.int32, (n_slc, tq), 0)
    tok = t0 + lax.broadcasted_iota(jnp.int32, (n_slc, tq), 1)
    cur = lax.shift_right_logical(tok, SLC_LEN.bit_length() - 1)
    forced = (blk == 0) | (blk == cur) | (blk == cur - 1)
    causal_blk = blk * SLC_LEN <= tok
    neg_t = []
    for g in groups:
        score = jnp.where(causal_blk, jnp.where(forced, FORCE_BONUS, imp_t[g]), NEG_INF)
        slabs = [score[r0:r0 + _SUBLANES] for r0 in range(0, n_slc, _SUBLANES)]
        ranks = [jnp.zeros((_SUBLANES, tq), jnp.int32) for _ in slabs]
        for i in range(n_slc):
            s_i = score[i:i + 1, :]
            for v, slab in enumerate(slabs):
                r0 = v * _SUBLANES
                if r0 > i:
                    ahead = s_i >= slab
                elif r0 + _SUBLANES - 1 <= i:
                    ahead = s_i > slab
                else:
                    ahead = (s_i > slab) | ((s_i == slab) & (blk[:_SUBLANES] > i - r0))
                ranks[v] = ranks[v] + jnp.where(ahead, 1, 0)
        rank = jnp.concatenate(ranks, axis=0)
        neg_t.append(jnp.where(rank < n_sel, 0.0, NEG_INF).astype(BF16))
    eye = eye_ref[...]
    neg = [_dot_nt(eye, neg_t[g]).astype(BF16) for g in groups]

    t_k = t0 + lax.broadcasted_iota(jnp.int32, (tq, tk), 0)
    c_k = lax.broadcasted_iota(jnp.int32, (tq, tk), 1)

    def slc_scores(kj):
        start = pl.multiple_of(kj * tk, tk)
        expand = exp_ref[kj]
        causal = c_k + kj * tk <= t_k
        madd = [jnp.where(causal, _dot(neg[g], expand), NEG_INF) for g in groups]
        return [_masked_scores(q4[g], ks_ref[0, g, pl.ds(start, tk), :], madd[g], r_heads) for g in groups]

    def slc_values(kj):
        start = pl.multiple_of(kj * tk, tk)
        return [vs_ref[0, g, pl.ds(start, tk), :] for g in groups]

    last = (t0 + tq - 1) // tk
    init = (jnp.full((rows, 1), NEG_INF, F32), jnp.zeros((rows, LANES), F32))
    states = lax.fori_loop(0, last, lambda kj, st: _online_softmax(slc_scores(kj), slc_values(kj), st),
                           (init,) * ng)

    start_w = pl.multiple_of(jnp.maximum(t0 + tq - wk, 0), tq)
    t_w = t0 + lax.broadcasted_iota(jnp.int32, (tq, wk), 0)
    key_w = start_w + lax.broadcasted_iota(jnp.int32, (tq, wk), 1)
    madd_w = jnp.where((key_w <= t_w) & (key_w > t_w - WINDOW), 0.0, NEG_INF)
    s_win = [_masked_scores(q4[g], kw_ref[0, g, pl.ds(start_w, wk), :], madd_w, r_heads) for g in groups]
    v_win = [vw_ref[0, g, pl.ds(start_w, wk), :] for g in groups]
    done = _online_softmax(slc_scores(last) + s_win, slc_values(last) + v_win, states + (init,) * ng)

    for g in groups:
        o_slc = _softmax_result(done[g])
        o_win = _softmax_result(done[ng + g])
        gates = gate_ref[0, :, g * LANES:(g + 1) * LANES]
        pieces = []
        for r in range(r_heads):
            sl = slice(r * tq, (r + 1) * tq)
            pieces.append(gates[:, r:r + 1] * o_cmp[g][sl]
                          + gates[:, r_heads + r:r_heads + r + 1] * o_slc[sl]
                          + gates[:, 2 * r_heads + r:2 * r_heads + r + 1] * o_win[sl])
        o_ref[0, :, g * gw:(g + 1) * gw] = jnp.concatenate(pieces, axis=1).astype(BF16)


def _nsa_attention(q, kk, vv, k_cmp, v_cmp, gates, tq=128, tk=512, ng=4):
    bsz, s, d = q.shape
    grp = NSA_KV_GROUPS
    gw = NSA_Q_PER_GROUP * HEAD_DIM
    n_cp = k_cmp.shape[2]
    n_cmp = (s - CMP_LEN) // CMP_STRIDE + 1
    n_slc = s // SLC_LEN
    n_sel = min(N_SELECT, n_slc)
    tq = min(tq, s)
    tk = min(tk, s)
    c0 = np.arange(n_cp)[None, :] * CMP_STRIDE
    s0 = np.arange(n_slc)[:, None] * SLC_LEN
    ov = np.maximum(np.minimum(c0 + CMP_LEN, s0 + SLC_LEN) - np.maximum(c0, s0), 0) / CMP_LEN
    ov = ov * (np.arange(n_cp)[None, :] < n_cmp)
    ovt = jnp.asarray(ov, BF16)
    eye = jnp.asarray(np.eye(tq, dtype=np.float32), BF16)
    n_kt = s // tk
    expand = (np.arange(n_slc)[None, :, None]
              == (np.arange(n_kt)[:, None, None] * (tk // SLC_LEN) + np.arange(tk)[None, None, :] // SLC_LEN))
    expand = jnp.asarray(expand.astype(np.float32), BF16)

    wk = min(-(-(WINDOW + tq - 1) // tq) * tq, s)

    def kv_spec(i, width):
        return pl.BlockSpec((1, 1, ng, s, width), lambda b, g, t, i=i: (b, i, g, 0, 0),
                            pipeline_mode=pl.Buffered(1))

    def kernel(q_ref, kc_ref, vc_ref, ks_ref, vs_ref, kw_ref, vw_ref, *rest):
        return _nsa_attn_kernel(q_ref, kc_ref, vc_ref, ks_ref.at[0], vs_ref.at[0], kw_ref.at[0],
                                vw_ref.at[0], *rest, tq=tq, tk=tk, wk=wk, n_cmp=n_cmp, n_sel=n_sel, ng=ng)

    return pl.pallas_call(
        kernel,
        out_shape=jax.ShapeDtypeStruct((bsz, s, d), BF16),
        grid=(bsz, grp // ng, s // tq),
        in_specs=[pl.BlockSpec((1, tq, ng * gw), lambda b, g, t: (b, t, g)),
                  pl.BlockSpec((1, ng, n_cp, HEAD_DIM), lambda b, g, t: (b, g, 0, 0)),
                  pl.BlockSpec((1, ng, n_cp, HEAD_DIM), lambda b, g, t: (b, g, 0, 0)),
                  kv_spec(0, HEAD_DIM), kv_spec(0, LANES), kv_spec(1, HEAD_DIM), kv_spec(1, LANES),
                  pl.BlockSpec((1, tq, ng * LANES), lambda b, g, t: (b, t, g)),
                  pl.BlockSpec((n_slc, n_cp), lambda b, g, t: (0, 0)),
                  pl.BlockSpec((tq, tq), lambda b, g, t: (0, 0)),
                  pl.BlockSpec((n_kt, n_slc, tk), lambda b, g, t: (0, 0, 0))],
        out_specs=pl.BlockSpec((1, tq, ng * gw), lambda b, g, t: (b, t, g)),
        compiler_params=_cparams(3),
        name="nsa_attention",
    )(q, k_cmp, v_cmp, kk, vv, kk, vv, gates, ovt, eye, expand)


_HALO = 32


_CONV_ROWS = 64


def _dwconv_kernel(x_ref, prev_ref, dw_ref, dwb_ref, lng_ref, lnb_ref, o_ref, ext_ref, z_ref, y_ref,
                   *, tm):
    i = pl.program_id(1)
    d = x_ref.shape[2]
    prev = prev_ref[0]
    ext_ref[0:_HALO, :] = jnp.where(i > 0, prev, jnp.zeros_like(prev))
    ext_ref[_HALO:_HALO + tm, :] = x_ref[0]
    lead = _HALO - (CONV_WIDTH - 1)
    offsets = range(lead, lead + CONV_WIDTH)
    rb = _CONV_ROWS
    for c0 in range(0, d, LANES):
        cols = slice(c0, c0 + LANES)

        def tap(r0, n, j, shift):
            return ext_ref[r0 + j - shift:r0 + j - shift + n, cols] * dw_ref[j - lead:j - lead + 1, cols]

        for s in range(1, _SUBLANES):
            group = [j for j in offsets if j % _SUBLANES == s]
            for r0 in range(0, tm + _SUBLANES, rb):
                n = min(rb, tm + _SUBLANES - r0)
                acc = tap(r0, n, group[0], s)
                for j in group[1:]:
                    acc = acc + tap(r0, n, j, s)
                z_ref[s - 1, r0:r0 + n, cols] = acc
        for r0 in range(0, tm, rb):
            acc = jnp.zeros((rb, LANES), F32) + dwb_ref[:, cols]
            for j in offsets:
                if j % _SUBLANES == 0:
                    acc = acc + tap(r0, rb, j, 0)
            for s in range(1, _SUBLANES):
                acc = acc + z_ref[s - 1, r0 + s:r0 + s + rb, cols]
            y_ref[r0:r0 + rb, cols] = acc
    acc = y_ref[...]
    mu = jnp.mean(acc, axis=-1, keepdims=True)
    cen = acc - mu
    var = jnp.mean(cen * cen, axis=-1, keepdims=True)
    y = cen * lax.rsqrt(var + NORM_EPS) * lng_ref[...] + lnb_ref[...]
    o_ref[0] = (y * jax.nn.sigmoid(y)).astype(BF16)


def _dwconv_ln_swish(x, dw, dw_b, ln_g, ln_b, tm=256):
    bsz, s, d = x.shape
    tm = min(tm, s)
    per = tm // _HALO
    return pl.pallas_call(
        functools.partial(_dwconv_kernel, tm=tm),
        out_shape=jax.ShapeDtypeStruct((bsz, s, d), BF16),
        grid=(bsz, s // tm),
        in_specs=[pl.BlockSpec((1, tm, d), lambda b, i: (b, i, 0)),
                  pl.BlockSpec((1, _HALO, d), lambda b, i: (b, jnp.maximum(i * per - 1, 0), 0)),
                  pl.BlockSpec((_HALO, d), lambda b, i: (0, 0)),
                  pl.BlockSpec((1, d), lambda b, i: (0, 0)),
                  pl.BlockSpec((1, d), lambda b, i: (0, 0)),
                  pl.BlockSpec((1, d), lambda b, i: (0, 0))],
        out_specs=pl.BlockSpec((1, tm, d), lambda b, i: (b, i, 0)),
        scratch_shapes=[pltpu.VMEM((_HALO + tm, d), F32),
                        pltpu.VMEM((_SUBLANES - 1, tm + _SUBLANES, d), F32),
                        pltpu.VMEM((tm, d), F32)],
        compiler_params=_cparams(2),
        name="dwconv_ln_swish",
    )(x, x, dw, dw_b, ln_g, ln_b)


def _rope_tables(positions):
    half = ROPE_DIM // 2
    inv_freq = ROPE_THETA ** (-jnp.arange(half, dtype=F32) / half)
    zeros = jnp.zeros((HEAD_DIM - ROPE_DIM,), F32)
    reps = LANES // HEAD_DIM
    freq = jnp.tile(jnp.concatenate([inv_freq, inv_freq, zeros]), reps)
    sign = jnp.tile(jnp.concatenate([-jnp.ones((half,), F32), jnp.ones((half,), F32), zeros]), reps)
    ang = positions.astype(F32)[..., None] * freq
    return jnp.cos(ang), jnp.sin(ang) * sign


def _nsa_weight(w_in, d):
    kv_end = d + 6 * NSA_KV_GROUPS * HEAD_DIM
    wg = w_in[:, kv_end:].reshape(d, NSA_KV_GROUPS, NSA_Q_PER_GROUP, 3)
    wg = wg.transpose(0, 1, 3, 2).reshape(d, NSA_KV_GROUPS, 3 * NSA_Q_PER_GROUP)
    wg = jnp.pad(wg, ((0, 0), (0, 0), (0, LANES - 3 * NSA_Q_PER_GROUP)))
    return jnp.concatenate([w_in[:, :kv_end], wg.reshape(d, NSA_KV_GROUPS * LANES)], axis=1).astype(BF16)


def kernel(x, c, positions, ada_w, ada_b, mix_pre_g, mix_post_g, ffn_pre_g, ffn_post_g, ffn_w1, ffn_w2, sb_w_in, sb_w_out, nsa_w_in, nsa_w_out, nsa_pe_k, nsa_w1_k, nsa_w2_k, nsa_pe_v, nsa_w1_v, nsa_w2_v, cv_w_in, cv_b_in, cv_dw, cv_dw_b, cv_ln_g, cv_ln_b, cv_w_out, cv_b_out):
    bsz, s, d = x.shape
    depth = ada_w.shape[0]
    n_mixers = 3
    mod = _ada_mod(c, ada_w, ada_b).reshape(depth, bsz, 6, 1, d)
    zero_bias = jnp.zeros((1, d), F32)
    h = x
    for i in range(depth):
        sh1, sc1, g1, sh2, sc2, g2 = [mod[i, :, m] for m in range(6)]
        pre_g = mix_pre_g[i].reshape(1, d)
        post_g = mix_post_g[i].reshape(1, d)
        kind, j = i % n_mixers, i // n_mixers
        if kind == 0:
            qkv = _sb_proj(h, pre_g, sc1, sh1, sb_w_in[j].astype(BF16))
            a = _sb_attention(qkv)
            w_out, b_out = sb_w_out[j], zero_bias
        elif kind == 1:
            cos_t, sin_t = _rope_tables(positions)
            q, kvc, kk, vv, gates = _nsa_proj(h, pre_g, sc1, sh1, _nsa_weight(nsa_w_in[j], d), cos_t, sin_t)
            end = jnp.minimum(jnp.arange(s // CMP_STRIDE) * CMP_STRIDE + CMP_LEN - 1, s - 1)
            cos_c, sin_c = _rope_tables(positions[:, end])
            k_cmp = _compress(kvc, 0, nsa_pe_k[j], nsa_w1_k[j], nsa_w2_k[j], cos_c, sin_c, True)
            v_cmp = _compress(kvc, 1, nsa_pe_v[j], nsa_w1_v[j], nsa_w2_v[j], cos_c, sin_c, False)
            a = _nsa_attention(q, kk, vv, k_cmp, v_cmp, gates)
            w_out, b_out = nsa_w_out[j], zero_bias
        else:
            a = _cv_proj(h, pre_g, sc1, sh1, cv_w_in[j].astype(BF16), cv_b_in[j].reshape(1, 2 * d))
            dw = jnp.pad(cv_dw[j].reshape(CONV_WIDTH, d), ((0, _HALO - CONV_WIDTH), (0, 0)))
            a = _dwconv_ln_swish(a, dw, cv_dw_b[j].reshape(1, d), cv_ln_g[j].reshape(1, d),
                                 cv_ln_b[j].reshape(1, d))
            w_out, b_out = cv_w_out[j], cv_b_out[j].reshape(1, d)
        h = _out_mlp(a, w_out.astype(BF16), b_out, h, g1, post_g, ffn_pre_g[i].reshape(1, d), sc2, sh2,
                     ffn_w1[i].astype(BF16), ffn_w2[i].astype(BF16), g2, ffn_post_g[i].reshape(1, d))
    return h
```

```python
import functools

import numpy as np
import jax
import jax.numpy as jnp
from jax import lax
from jax.experimental import pallas as pl
from jax.experimental.pallas import tpu as pltpu

F32 = jnp.float32
BF16 = jnp.bfloat16

N_HEADS = 16
HEAD_DIM = 64
ROPE_THETA = 500000.0
ROPE_DIM = HEAD_DIM // 4
NORM_EPS = 1e-6
NSA_KV_GROUPS = 4
NSA_Q_PER_GROUP = N_HEADS // NSA_KV_GROUPS
CMP_LEN = 32
CMP_STRIDE = 16
SLC_LEN = 64
N_SELECT = 16
WINDOW = 512
FORCE_BONUS = 1e4
NEG_INF = -1e30
CONV_WIDTH = 31
ATTN_SCALE = HEAD_DIM ** -0.5
LOG2_E = 1.4426950408889634

LANES = 128
_SUBLANES = 8
VMEM_LIMIT_BYTES = 56 * 1024 * 1024

_NT = (((1,), (1,)), ((), ()))


def _cparams(n_axes):
    return pltpu.CompilerParams(dimension_semantics=("arbitrary",) * n_axes,
                                vmem_limit_bytes=VMEM_LIMIT_BYTES)


def _dot(a, b):
    return jnp.dot(a, b, preferred_element_type=F32)


def _dot_nt(a, b):
    return lax.dot_general(a, b, _NT, preferred_element_type=F32)


def _split_bf16(x):
    hi = x.astype(BF16)
    lo = (x - hi.astype(F32)).astype(BF16)
    return hi, lo


def _rms(x):
    return x * lax.rsqrt(jnp.mean(x * x, axis=-1, keepdims=True) + NORM_EPS)


def _norm_mod(h, g, sc, sh):
    return (_rms(h) * g) * (1.0 + sc) + sh


def _rope(x, cos_t, sin_t):
    rows, w = x.shape
    reps = w // LANES
    cos_w = jnp.concatenate([cos_t] * reps, axis=1) if reps > 1 else cos_t
    sin_w = jnp.concatenate([sin_t] * reps, axis=1) if reps > 1 else sin_t
    half = ROPE_DIM // 2
    lane = lax.broadcasted_iota(jnp.int32, (rows, w), 1)
    first_half = (lane & (HEAD_DIM - 1)) < half
    partner = jnp.where(first_half, pltpu.roll(x, w - half, 1), pltpu.roll(x, half, 1))
    return x * cos_w + partner * sin_w


def _ada_kernel(c_ref, w_ref, b_ref, o_ref):
    c = c_ref[...]
    cond = c * jax.nn.sigmoid(c)
    o_ref[0] = jnp.dot(cond, w_ref[0], preferred_element_type=F32,
                       precision=lax.Precision.HIGHEST) + b_ref[0]


def _ada_mod(c, ada_w, ada_b):
    depth, d, n = ada_w.shape
    b = c.shape[0]
    tn = 1024
    return pl.pallas_call(
        _ada_kernel,
        out_shape=jax.ShapeDtypeStruct((depth, b, n), F32),
        grid=(depth, n // tn),
        in_specs=[pl.BlockSpec((b, d), lambda i, j: (0, 0)),
                  pl.BlockSpec((1, d, tn), lambda i, j: (i, 0, j)),
                  pl.BlockSpec((1, 1, tn), lambda i, j: (i, 0, j))],
        out_specs=pl.BlockSpec((1, b, tn), lambda i, j: (i, 0, j)),
        compiler_params=_cparams(2),
        name="ada_mod",
    )(c, ada_w, ada_b.reshape(depth, 1, n))


def _proj_in_specs(tm, d, n_w):
    return [pl.BlockSpec((1, tm, d), lambda b, i: (b, i, 0)),
            pl.BlockSpec((1, d), lambda b, i: (0, 0)),
            pl.BlockSpec((1, 1, d), lambda b, i: (b, 0, 0)),
            pl.BlockSpec((1, 1, d), lambda b, i: (b, 0, 0)),
            pl.BlockSpec((d, n_w), lambda b, i: (0, 0))]


def _sb_proj_kernel(h_ref, g_ref, sc_ref, sh_ref, w_ref, o_ref, *, tn):
    d = h_ref.shape[2]
    u = _norm_mod(h_ref[0], g_ref[...], sc_ref[0], sh_ref[0]).astype(BF16)
    for j in range(w_ref.shape[1] // tn):
        acc = _dot(u, w_ref[:, j * tn:(j + 1) * tn])
        if j * tn < d:
            acc = acc * (-ATTN_SCALE)
        o_ref[0, :, j * tn:(j + 1) * tn] = acc.astype(BF16)


def _sb_proj(h, g, sc, sh, w, tm=512, tn=512):
    bsz, s, d = h.shape
    n = w.shape[1]
    return pl.pallas_call(
        functools.partial(_sb_proj_kernel, tn=tn),
        out_shape=jax.ShapeDtypeStruct((bsz, s, n), BF16),
        grid=(bsz, s // tm),
        in_specs=_proj_in_specs(tm, d, n),
        out_specs=pl.BlockSpec((1, tm, n), lambda b, i: (b, i, 0)),
        compiler_params=_cparams(2),
        name="sb_proj",
    )(h, g, sc, sh, w)


def _nsa_proj_kernel(h_ref, g_ref, sc_ref, sh_ref, w_ref, cos_ref, sin_ref,
                     q_ref, kvc_ref, kk_ref, vv_ref, gate_ref, *, tn):
    d = h_ref.shape[2]
    tm = h_ref.shape[1]
    kv_w = NSA_KV_GROUPS * HEAD_DIM
    u = _norm_mod(h_ref[0], g_ref[...], sc_ref[0], sh_ref[0]).astype(BF16)
    cos_t = cos_ref[0]
    sin_t = sin_ref[0]
    one_col = jnp.where(lax.broadcasted_iota(jnp.int32, (tm, LANES - HEAD_DIM), 1) == 0, 1.0, 0.0)
    n_q = d // tn
    n_kv = 6 * kv_w // tn
    for j in range(w_ref.shape[1] // tn):
        acc = _dot(u, w_ref[:, j * tn:(j + 1) * tn])
        if j < n_q:
            q = _rope(acc, cos_t, sin_t) * ATTN_SCALE
            q_ref[0, :, j * tn:(j + 1) * tn] = q.astype(BF16)
        elif j < n_q + n_kv:
            for part in range(tn // kv_w):
                i = (j - n_q) * (tn // kv_w) + part
                x = acc[:, part * kv_w:(part + 1) * kv_w]
                if i in (2, 4):
                    x = _rope(x, cos_t, sin_t)
                for grp in range(NSA_KV_GROUPS):
                    xg = x[:, grp * HEAD_DIM:(grp + 1) * HEAD_DIM]
                    if i < 2:
                        kvc_ref[0, i, grp] = xg
                    elif i in (2, 4):
                        kk_ref[0, (i - 2) // 2, grp] = xg.astype(BF16)
                    else:
                        vv_ref[0, (i - 3) // 2, grp] = jnp.concatenate([xg, one_col], axis=1).astype(BF16)
        else:
            c0 = (j - n_q - n_kv) * tn
            gate_ref[0, :, c0:c0 + tn] = jax.nn.sigmoid(acc)


def _nsa_proj(h, g, sc, sh, w, cos_t, sin_t, tm=512, tn=256):
    bsz, s, d = h.shape
    n = w.shape[1]
    n_gate = NSA_KV_GROUPS * LANES
    return pl.pallas_call(
        functools.partial(_nsa_proj_kernel, tn=tn),
        out_shape=(jax.ShapeDtypeStruct((bsz, s, d), BF16),
                   jax.ShapeDtypeStruct((bsz, 2, NSA_KV_GROUPS, s, HEAD_DIM), F32),
                   jax.ShapeDtypeStruct((bsz, 2, NSA_KV_GROUPS, s, HEAD_DIM), BF16),
                   jax.ShapeDtypeStruct((bsz, 2, NSA_KV_GROUPS, s, LANES), BF16),
                   jax.ShapeDtypeStruct((bsz, s, n_gate), F32)),
        grid=(bsz, s // tm),
        in_specs=_proj_in_specs(tm, d, n) + [
            pl.BlockSpec((1, tm, LANES), lambda b, i: (b, i, 0)),
            pl.BlockSpec((1, tm, LANES), lambda b, i: (b, i, 0))],
        out_specs=(pl.BlockSpec((1, tm, d), lambda b, i: (b, i, 0)),
                   pl.BlockSpec((1, 2, NSA_KV_GROUPS, tm, HEAD_DIM), lambda b, i: (b, 0, 0, i, 0)),
                   pl.BlockSpec((1, 2, NSA_KV_GROUPS, tm, HEAD_DIM), lambda b, i: (b, 0, 0, i, 0)),
                   pl.BlockSpec((1, 2, NSA_KV_GROUPS, tm, LANES), lambda b, i: (b, 0, 0, i, 0)),
                   pl.BlockSpec((1, tm, n_gate), lambda b, i: (b, i, 0))),
        compiler_params=_cparams(2),
        name="nsa_proj",
    )(h, g, sc, sh, w, cos_t, sin_t)


def _cv_proj_kernel(h_ref, g_ref, sc_ref, sh_ref, w_ref, b_ref, o_ref, *, tn):
    d = h_ref.shape[2]
    u = _norm_mod(h_ref[0], g_ref[...], sc_ref[0], sh_ref[0]).astype(BF16)
    for j in range(d // tn):
        a = _dot(u, w_ref[:, j * tn:(j + 1) * tn]) + b_ref[:, j * tn:(j + 1) * tn]
        gt = _dot(u, w_ref[:, d + j * tn:d + (j + 1) * tn]) + b_ref[:, d + j * tn:d + (j + 1) * tn]
        o_ref[0, :, j * tn:(j + 1) * tn] = a * jax.nn.sigmoid(gt)


def _cv_proj(h, g, sc, sh, w, bias, tm=512, tn=256):
    bsz, s, d = h.shape
    n = w.shape[1]
    return pl.pallas_call(
        functools.partial(_cv_proj_kernel, tn=tn),
        out_shape=jax.ShapeDtypeStruct((bsz, s, d), F32),
        grid=(bsz, s // tm),
        in_specs=_proj_in_specs(tm, d, n) + [pl.BlockSpec((1, n), lambda b, i: (0, 0))],
        out_specs=pl.BlockSpec((1, tm, d), lambda b, i: (b, i, 0)),
        compiler_params=_cparams(2),
        name="cv_proj",
    )(h, g, sc, sh, w, bias)


def _out_mlp_kernel(a_ref, wo_ref, bo_ref, h_ref, g1_ref, pg1_ref, pre_ref, sc_ref, sh_ref,
                    w1_ref, w2_ref, g2_ref, pg2_ref, o_ref, h1_ref, hid_ref, *, tf, tn):
    a = a_ref[0]
    d = wo_ref.shape[1]
    d_ff = w1_ref.shape[1]
    for j in range(d // tn):
        o_ref[0, :, j * tn:(j + 1) * tn] = (_dot(a, wo_ref[:, j * tn:(j + 1) * tn])
                                            + bo_ref[:, j * tn:(j + 1) * tn])
    h1_ref[...] = h_ref[0] + g1_ref[0] * (_rms(o_ref[0]) * pg1_ref[...])
    u = _norm_mod(h1_ref[...], pre_ref[...], sc_ref[0], sh_ref[0]).astype(BF16)
    for c in range(d_ff // tf):
        act = jnp.maximum(_dot(u, w1_ref[:, c * tf:(c + 1) * tf]), 0.0)
        hid_ref[:, c * tf:(c + 1) * tf] = (act * act).astype(BF16)
    for j in range(d // tn):
        o_ref[0, :, j * tn:(j + 1) * tn] = _dot(hid_ref[...], w2_ref[:, j * tn:(j + 1) * tn])
    o_ref[0] = h1_ref[...] + g2_ref[0] * (_rms(o_ref[0]) * pg2_ref[...])


def _out_mlp(a, w_out, b_out, h, g1, post_g1, pre_g2, sc2, sh2, w1, w2, g2, post_g2,
             tm=512, tf=512, tn=256):
    bsz, s, k = a.shape
    d = w_out.shape[1]
    d_ff = w1.shape[1]

    def rows(width):
        return pl.BlockSpec((1, tm, width), lambda b, i: (b, i, 0))

    def per_batch():
        return pl.BlockSpec((1, 1, d), lambda b, i: (b, 0, 0))

    def whole(shape, single=False):
        return pl.BlockSpec(shape, lambda b, i: (0, 0), pipeline_mode=pl.Buffered(1) if single else None)

    return pl.pallas_call(
        functools.partial(_out_mlp_kernel, tf=tf, tn=tn),
        out_shape=jax.ShapeDtypeStruct((bsz, s, d), F32),
        grid=(bsz, s // tm),
        in_specs=[rows(k), whole((k, d), True), whole((1, d)), rows(d), per_batch(), whole((1, d)),
                  whole((1, d)), per_batch(), per_batch(),
                  whole((d, d_ff), True), whole((d_ff, d), True), per_batch(), whole((1, d))],
        out_specs=rows(d),
        scratch_shapes=[pltpu.VMEM((tm, d), F32), pltpu.VMEM((tm, d_ff), BF16)],
        compiler_params=_cparams(2),
        name="out_mlp",
    )(a, w_out, b_out, h, g1, post_g1, pre_g2, sc2, sh2, w1, w2, g2, post_g2)


SB_ZERO_WEIGHT_LOG = -110.0


def _log_keep(w):
    return jnp.minimum(w, 0.0) - jnp.log(1.0 + jnp.exp2(jnp.abs(w) * (-LOG2_E)))


def _sb_tiles(qs, ks, vs, tri2, state):
    n = len(qs)
    ws, lks, hls, tails, probs, out = {}, {}, {}, {}, {}, {}
    for step in range(n + 4):
        if step < n:
            ws[step] = _dot_nt(qs[step], ks[step])
        h = step - 1
        if 0 <= h < n:
            lks[h] = _log_keep(ws[h])
            hi, lo = _split_bf16(lks[h])
            hls[h] = jnp.concatenate([hi, lo], axis=1)
        h = step - 2
        if 0 <= h < n:
            tails[h] = _dot(hls[h], tri2)
        h = step - 3
        if 0 <= h < n:
            probs[h] = jnp.exp((tails[h] - ws[h]) + state[2 * h]).astype(BF16)
        h = step - 4
        if 0 <= h < n:
            out[h] = (state[2 * h] + jnp.sum(lks[h], axis=-1, keepdims=True),
                      state[2 * h + 1] + _dot(probs[h], vs[h]))
    return sum((out[h] for h in range(n)), ())


def _sb_diag_tiles(qs, ks, vs, tri2, prev=None):
    n = len(qs)
    hf = qs[0].shape[0] // 2
    strict = (lax.broadcasted_iota(jnp.int32, (hf, hf), 1) < lax.broadcasted_iota(jnp.int32, (hf, hf), 0))
    zero = jnp.zeros((hf, hf), BF16)

    def assemble(tl, bl, br):
        return jnp.concatenate([jnp.concatenate([tl, zero], axis=1), jnp.concatenate([bl, br], axis=1)], axis=0)

    ws = [_dot_nt(qs[h], ks[h]) for h in range(n)]
    ws_p = [_dot_nt(qs[h], prev[0][h]) for h in range(n)] if prev else []
    quads, hls, carries = [], [], []
    for w in ws:
        w_tl, w_bl, w_br = w[:hf, :hf], w[hf:, :hf], w[hf:, hf:]
        lk_tl = jnp.where(strict, _log_keep(w_tl), 0.0)
        lk_bl = _log_keep(w_bl)
        lk_br = jnp.where(strict, _log_keep(w_br), 0.0)
        his, los = zip(_split_bf16(lk_tl), _split_bf16(lk_bl), _split_bf16(lk_br))
        hls.append(jnp.concatenate([assemble(*his), assemble(*los)], axis=1))
        carries.append(jnp.concatenate(
            [jnp.sum(lk_tl, axis=-1, keepdims=True),
             jnp.sum(lk_bl, axis=-1, keepdims=True) + jnp.sum(lk_br, axis=-1, keepdims=True)], axis=0))
        quads.append((w_tl, w_bl, w_br))
    lks_p = [_log_keep(w) for w in ws_p]
    hls_p = [jnp.concatenate(_split_bf16(lk), axis=1) for lk in lks_p]
    tails = [_dot(hl, tri2) for hl in hls]
    tails_p = [_dot(hl, tri2) for hl in hls_p]
    probs = []
    for h in range(n):
        w_tl, w_bl, w_br = quads[h]
        tl = tails[h]
        probs.append(assemble(jnp.where(strict, jnp.exp(tl[:hf, :hf] - w_tl), 0.0).astype(BF16),
                              jnp.exp(tl[hf:, :hf] - w_bl).astype(BF16),
                              jnp.where(strict, jnp.exp(tl[hf:, hf:] - w_br), 0.0).astype(BF16)))
    probs_p = [jnp.exp((tails_p[h] - ws_p[h]) + carries[h]).astype(BF16) for h in range(len(ws_p))]
    out = ()
    for h in range(n):
        carry, acc = carries[h], _dot(probs[h], vs[h])
        if prev:
            carry = carry + jnp.sum(lks_p[h], axis=-1, keepdims=True)
            acc = acc + _dot(probs_p[h], prev[1][h])
        out += (carry, acc)
    return out


def _sb_attn_kernel(q_ref, k_ref, v_ref, tri_ref, o_ref, *, t, pairs):
    qi = pl.program_id(2)
    tri2 = tri_ref[...]
    lane = lax.broadcasted_iota(jnp.int32, (t, LANES), 1)
    q_heads = []
    for p in range(pairs):
        q = q_ref[0, :, p * LANES:(p + 1) * LANES]
        zero_q = jnp.zeros_like(q)
        q_heads += [jnp.where(lane < HEAD_DIM, q, zero_q),
                    jnp.where(lane >= HEAD_DIM, q, zero_q)]
    n_heads = 2 * pairs

    def key_tile(kj):
        start = pl.multiple_of(kj * t, t)
        cols = [slice((hd // 2) * LANES, (hd // 2 + 1) * LANES) for hd in range(n_heads)]
        return [k_ref[0, pl.ds(start, t), c] for c in cols], [v_ref[0, pl.ds(start, t), c] for c in cols]

    def largest(state):
        carries = state[0::2]
        top = carries[0]
        for c in carries[1:]:
            top = jnp.maximum(top, c)
        return jnp.max(top)

    state = lax.cond(qi > 0,
                     lambda: _sb_diag_tiles(q_heads, *key_tile(qi), tri2, key_tile(jnp.maximum(qi - 1, 0))),
                     lambda: _sb_diag_tiles(q_heads, *key_tile(qi), tri2))

    def cond(loop):
        return (loop[0] < qi) & (loop[1] > SB_ZERO_WEIGHT_LOG)

    def body(loop):
        state = _sb_tiles(q_heads, *key_tile(qi - 1 - loop[0]), tri2, loop[2:])
        return (loop[0] + 1, largest(state)) + state

    out = lax.while_loop(cond, body, (jnp.int32(1), largest(state)) + state)[2:]
    for p in range(pairs):
        o_ref[0, :, p * LANES:(p + 1) * LANES] = jnp.where(
            lane < HEAD_DIM, out[4 * p + 1], out[4 * p + 3]).astype(BF16)


def _sb_attention(qkv, t=256, pairs=2):
    bsz, s, n3 = qkv.shape
    d = n3 // 3
    w = pairs * LANES
    n_grp = d // w
    t = min(t, s)
    tri = np.tril(np.ones((t, t), np.float32))
    tri = jnp.asarray(np.concatenate([tri, tri], axis=0), BF16)
    return pl.pallas_call(
        functools.partial(_sb_attn_kernel, t=t, pairs=pairs),
        out_shape=jax.ShapeDtypeStruct((bsz, s, d), BF16),
        grid=(bsz, n_grp, s // t),
        in_specs=[pl.BlockSpec((1, t, w), lambda b, p, i: (b, i, p)),
                  pl.BlockSpec((1, s, w), lambda b, p, i: (b, 0, n_grp + p)),
                  pl.BlockSpec((1, s, w), lambda b, p, i: (b, 0, 2 * n_grp + p)),
                  pl.BlockSpec((2 * t, t), lambda b, p, i: (0, 0))],
        out_specs=pl.BlockSpec((1, t, w), lambda b, p, i: (b, i, p)),
        compiler_params=_cparams(3),
        name="sb_attention",
    )(qkv, qkv, qkv, tri)


def _compress_kernel(x_ref, pe_ref, w1_ref, w2_ref, cos_ref, sin_ref, o_ref, *, rope):
    w1 = w1_ref[...]
    hid = w1.shape[1] // 2
    n_seg = o_ref.shape[2]
    pre = jnp.zeros((n_seg, 2 * hid), F32)
    for tok in range(CMP_STRIDE):
        x_tok = x_ref[0, 0, 0, pl.ds(tok, n_seg, stride=CMP_STRIDE), :].astype(BF16)
        pre = pre + _dot(x_tok, w1[tok * HEAD_DIM:(tok + 1) * HEAD_DIM, :])
    pe_term = _dot(pe_ref[...], w1)
    bias = pe_term[0:1, :hid] + pe_term[8:9, hid:]
    nxt = pltpu.roll(pre[:, hid:], n_seg - 1, 0)
    mid = jax.nn.gelu(pre[:, :hid] + nxt + bias)
    out = _dot(mid.astype(BF16), w2_ref[...])
    if rope:
        out = _rope(out, cos_ref[0], sin_ref[0])
    o_ref[0, 0] = out[:, :HEAD_DIM].astype(BF16)


def _compress(kvc, which, pe, w1, w2, cos_c, sin_c, rope):
    bsz, _, grp, s, _ = kvc.shape
    n_seg = s // CMP_STRIDE
    hid = w1.shape[1]
    half = w1.shape[0] // 2
    w1cat = jnp.concatenate([w1[:half], w1[half:]], axis=1).astype(BF16)
    w2p = jnp.pad(w2, ((0, 0), (0, LANES - HEAD_DIM))).astype(BF16)
    pe_flat = pe.reshape(2, half)
    pe_rows = jnp.zeros((16, half), F32).at[0].set(pe_flat[0]).at[8].set(pe_flat[1]).astype(BF16)
    return pl.pallas_call(
        functools.partial(_compress_kernel, rope=rope),
        out_shape=jax.ShapeDtypeStruct((bsz, grp, n_seg, HEAD_DIM), BF16),
        grid=(bsz, grp),
        in_specs=[pl.BlockSpec((1, 1, 1, s, HEAD_DIM), lambda b, g: (b, which, g, 0, 0)),
                  pl.BlockSpec((16, half), lambda b, g: (0, 0)),
                  pl.BlockSpec((half, 2 * hid), lambda b, g: (0, 0)),
                  pl.BlockSpec((hid, LANES), lambda b, g: (0, 0)),
                  pl.BlockSpec((1, n_seg, LANES), lambda b, g: (b, 0, 0)),
                  pl.BlockSpec((1, n_seg, LANES), lambda b, g: (b, 0, 0))],
        out_specs=pl.BlockSpec((1, 1, n_seg, HEAD_DIM), lambda b, g: (b, g, 0, 0)),
        compiler_params=_cparams(2),
        name="nsa_compress",
    )(kvc, pe_rows, w1cat, w2p, cos_c, sin_c)


def _masked_scores(q4, k, madd, r_heads):
    rows, tk = q4.shape[0], k.shape[0]
    s = _dot_nt(q4, k).reshape(r_heads, rows // r_heads, tk) + madd[None]
    return s.reshape(rows, tk)


def _online_softmax(scores, values, states):
    stats = []
    for s, (m, _) in zip(scores, states):
        m_new = jnp.maximum(m, jnp.max(s, axis=-1, keepdims=True))
        stats.append((m_new, jnp.exp(m - m_new), jnp.exp(s - m_new).astype(BF16)))
    pv = [_dot(st[2], v) for st, v in zip(stats, values)]
    return tuple((st[0], st[1] * state[1] + o) for st, state, o in zip(stats, states, pv))


def _softmax_result(state):
    acc = state[1]
    return acc[:, :HEAD_DIM] / acc[:, HEAD_DIM:HEAD_DIM + 1]


def _nsa_attn_kernel(q_ref, kc_ref, vc_ref, ks_ref, vs_ref, kw_ref, vw_ref, gate_ref,
                     ovt_ref, eye_ref, exp_ref, o_ref, *, tq, tk, wk, n_cmp, n_sel, ng):
    qi = pl.program_id(2)
    r_heads = NSA_Q_PER_GROUP
    rows = r_heads * tq
    gw = r_heads * HEAD_DIM
    t0 = qi * tq
    n_cp = kc_ref.shape[2]
    n_slc = ovt_ref.shape[0]
    groups = range(ng)

    q4 = []
    for g in groups:
        qf = q_ref[0, :, g * gw:(g + 1) * gw].astype(F32)
        q4.append(jnp.concatenate([qf[:, r * HEAD_DIM:(r + 1) * HEAD_DIM] for r in range(r_heads)],
                                  axis=0).astype(BF16))

    t_c = t0 + lax.broadcasted_iota(jnp.int32, (tq, n_cp), 0)
    n_c = lax.broadcasted_iota(jnp.int32, (tq, n_cp), 1)
    cmp_ok = (n_c * CMP_STRIDE + (CMP_LEN - 1) <= t_c) & (n_c < n_cmp)
    any_ok = (t_c[:, 0:1] >= CMP_LEN - 1).astype(F32)
    sc = [_dot_nt(q4[g], kc_ref[0, g]).reshape(r_heads, tq, n_cp) for g in groups]
    p_cmp = []
    for g in groups:
        s = jnp.where(cmp_ok[None], sc[g], NEG_INF)
        e = jnp.exp(s - jnp.max(s, axis=-1, keepdims=True))
        p_cmp.append(e * (any_ok[None] / jnp.sum(e, axis=-1, keepdims=True)))
    o_cmp = [_dot(p_cmp[g].reshape(rows, n_cp).astype(BF16), vc_ref[0, g]) for g in groups]

    ovt = ovt_ref[...]
    imp_t = []
    for g in groups:
        p_hi, p_lo = _split_bf16(jnp.sum(p_cmp[g], axis=0))
        imp_t.append(_dot_nt(ovt, p_hi) + _dot_nt(ovt, p_lo))
    blk = lax.broadcasted_iota(jnp.int32, (n_slc, tq), 0)
    tok = t0 + lax.broadcasted_iota(jnp.int32, (n_slc, tq), 1)
    cur = lax.shift_right_logical(tok, SLC_LEN.bit_length() - 1)
    forced = (blk == 0) | (blk == cur) | (blk == cur - 1)
    causal_blk = blk * SLC_LEN <= tok
    neg_t = []
    for g in groups:
        score = jnp.where(causal_blk, jnp.where(forced, FORCE_BONUS, imp_t[g]), NEG_INF)
        slabs = [score[r0:r0 + _SUBLANES] for r0 in range(0, n_slc, _SUBLANES)]
        ranks = [jnp.zeros((_SUBLANES, tq), jnp.int32) for _ in slabs]
        for i in range(n_slc):
            s_i = score[i:i + 1, :]
            for v, slab in enumerate(slabs):
                r0 = v * _SUBLANES
                if r0 > i:
                    ahead = s_i >= slab
                elif r0 + _SUBLANES - 1 <= i:
                    ahead = s_i > slab
                else:
                    ahead = (s_i > slab) | ((s_i == slab) & (blk[:_SUBLANES] > i - r0))
                ranks[v] = ranks[v] + jnp.where(ahead, 1, 0)
        rank = jnp.concatenate(ranks, axis=0)
        neg_t.append(jnp.where(rank < n_sel, 0.0, NEG_INF).astype(BF16))
    eye = eye_ref[...]
    neg = [_dot_nt(eye, neg_t[g]).astype(BF16) for g in groups]

    t_k = t0 + lax.broadcasted_iota(jnp.int32, (tq, tk), 0)
    c_k = lax.broadcasted_iota(jnp.int32, (tq, tk), 1)

    def slc_scores(kj):
        start = pl.multiple_of(kj * tk, tk)
        expand = exp_ref[kj]
        causal = c_k + kj * tk <= t_k
        madd = [jnp.where(causal, _dot(neg[g], expand), NEG_INF) for g in groups]
        return [_masked_scores(q4[g], ks_ref[0, g, pl.ds(start, tk), :], madd[g], r_heads) for g in groups]

    def slc_values(kj):
        start = pl.multiple_of(kj * tk, tk)
        return [vs_ref[0, g, pl.ds(start, tk), :] for g in groups]

    last = (t0 + tq - 1) // tk
    init = (jnp.full((rows, 1), NEG_INF, F32), jnp.zeros((rows, LANES), F32))
    states = lax.fori_loop(0, last, lambda kj, st: _online_softmax(slc_scores(kj), slc_values(kj), st),
                           (init,) * ng)

    start_w = pl.multiple_of(jnp.maximum(t0 + tq - wk, 0), tq)
    t_w = t0 + lax.broadcasted_iota(jnp.int32, (tq, wk), 0)
    key_w = start_w + lax.broadcasted_iota(jnp.int32, (tq, wk), 1)
    madd_w = jnp.where((key_w <= t_w) & (key_w > t_w - WINDOW), 0.0, NEG_INF)
    s_win = [_masked_scores(q4[g], kw_ref[0, g, pl.ds(start_w, wk), :], madd_w, r_heads) for g in groups]
    v_win = [vw_ref[0, g, pl.ds(start_w, wk), :] for g in groups]
    done = _online_softmax(slc_scores(last) + s_win, slc_values(last) + v_win, states + (init,) * ng)

    for g in groups:
        o_slc = _softmax_result(done[g])
        o_win = _softmax_result(done[ng + g])
        gates = gate_ref[0, :, g * LANES:(g + 1) * LANES]
        pieces = []
        for r in range(r_heads):
            sl = slice(r * tq, (r + 1) * tq)
            pieces.append(gates[:, r:r + 1] * o_cmp[g][sl]
                          + gates[:, r_heads + r:r_heads + r + 1] * o_slc[sl]
                          + gates[:, 2 * r_heads + r:2 * r_heads + r + 1] * o_win[sl])
        o_ref[0, :, g * gw:(g + 1) * gw] = jnp.concatenate(pieces, axis=1).astype(BF16)


def _nsa_attention(q, kk, vv, k_cmp, v_cmp, gates, tq=128, tk=512, ng=4):
    bsz, s, d = q.shape
    grp = NSA_KV_GROUPS
    gw = NSA_Q_PER_GROUP * HEAD_DIM
    n_cp = k_cmp.shape[2]
    n_cmp = (s - CMP_LEN) // CMP_STRIDE + 1
    n_slc = s // SLC_LEN
    n_sel = min(N_SELECT, n_slc)
    tq = min(tq, s)
    tk = min(tk, s)
    c0 = np.arange(n_cp)[None, :] * CMP_STRIDE
    s0 = np.arange(n_slc)[:, None] * SLC_LEN
    ov = np.maximum(np.minimum(c0 + CMP_LEN, s0 + SLC_LEN) - np.maximum(c0, s0), 0) / CMP_LEN
    ov = ov * (np.arange(n_cp)[None, :] < n_cmp)
    ovt = jnp.asarray(ov, BF16)
    eye = jnp.asarray(np.eye(tq, dtype=np.float32), BF16)
    n_kt = s // tk
    expand = (np.arange(n_slc)[None, :, None]
              == (np.arange(n_kt)[:, None, None] * (tk // SLC_LEN) + np.arange(tk)[None, None, :] // SLC_LEN))
    expand = jnp.asarray(expand.astype(np.float32), BF16)

    wk = min(-(-(WINDOW + tq - 1) // tq) * tq, s)

    def kv_spec(i, width):
        return pl.BlockSpec((1, 1, ng, s, width), lambda b, g, t, i=i: (b, i, g, 0, 0),
                            pipeline_mode=pl.Buffered(1))

    def kernel(q_ref, kc_ref, vc_ref, ks_ref, vs_ref, kw_ref, vw_ref, *rest):
        return _nsa_attn_kernel(q_ref, kc_ref, vc_ref, ks_ref.at[0], vs_ref.at[0], kw_ref.at[0],
                                vw_ref.at[0], *rest, tq=tq, tk=tk, wk=wk, n_cmp=n_cmp, n_sel=n_sel, ng=ng)

    return pl.pallas_call(
        kernel,
        out_shape=jax.ShapeDtypeStruct((bsz, s, d), BF16),
        grid=(bsz, grp // ng, s // tq),
        in_specs=[pl.BlockSpec((1, tq, ng * gw), lambda b, g, t: (b, t, g)),
                  pl.BlockSpec((1, ng, n_cp, HEAD_DIM), lambda b, g, t: (b, g, 0, 0)),
                  pl.BlockSpec((1, ng, n_cp, HEAD_DIM), lambda b, g, t: (b, g, 0, 0)),
                  kv_spec(0, HEAD_DIM), kv_spec(0, LANES), kv_spec(1, HEAD_DIM), kv_spec(1, LANES),
                  pl.BlockSpec((1, tq, ng * LANES), lambda b, g, t: (b, t, g)),
                  pl.BlockSpec((n_slc, n_cp), lambda b, g, t: (0, 0)),
                  pl.BlockSpec((tq, tq), lambda b, g, t: (0, 0)),
                  pl.BlockSpec((n_kt, n_slc, tk), lambda b, g, t: (0, 0, 0))],
        out_specs=pl.BlockSpec((1, tq, ng * gw), lambda b, g, t: (b, t, g)),
        compiler_params=_cparams(3),
        name="nsa_attention",
    )(q, k_cmp, v_cmp, kk, vv, kk, vv, gates, ovt, eye, expand)


_HALO = 32


_CONV_ROWS = 64


def _dwconv_kernel(x_ref, prev_ref, dw_ref, dwb_ref, lng_ref, lnb_ref, o_ref, ext_ref, z_ref, y_ref,
                   *, tm):
    i = pl.program_id(1)
    d = x_ref.shape[2]
    prev = prev_ref[0]
    ext_ref[0:_HALO, :] = jnp.where(i > 0, prev, jnp.zeros_like(prev))
    ext_ref[_HALO:_HALO + tm, :] = x_ref[0]
    lead = _HALO - (CONV_WIDTH - 1)
    offsets = range(lead, lead + CONV_WIDTH)
    rb = _CONV_ROWS
    for c0 in range(0, d, LANES):
        cols = slice(c0, c0 + LANES)

        def tap(r0, n, j, shift):
            return ext_ref[r0 + j - shift:r0 + j - shift + n, cols] * dw_ref[j - lead:j - lead + 1, cols]

        for s in range(1, _SUBLANES):
            group = [j for j in offsets if j % _SUBLANES == s]
            for r0 in range(0, tm + _SUBLANES, rb):
                n = min(rb, tm + _SUBLANES - r0)
                acc = tap(r0, n, group[0], s)
                for j in group[1:]:
                    acc = acc + tap(r0, n, j, s)
                z_ref[s - 1, r0:r0 + n, cols] = acc
        for r0 in range(0, tm, rb):
            acc = jnp.zeros((rb, LANES), F32) + dwb_ref[:, cols]
            for j in offsets:
                if j % _SUBLANES == 0:
                    acc = acc + tap(r0, rb, j, 0)
            for s in range(1, _SUBLANES):
                acc = acc + z_ref[s - 1, r0 + s:r0 + s + rb, cols]
            y_ref[r0:r0 + rb, cols] = acc
    acc = y_ref[...]
    mu = jnp.mean(acc, axis=-1, keepdims=True)
    cen = acc - mu
    var = jnp.mean(cen * cen, axis=-1, keepdims=True)
    y = cen * lax.rsqrt(var + NORM_EPS) * lng_ref[...] + lnb_ref[...]
    o_ref[0] = (y * jax.nn.sigmoid(y)).astype(BF16)


def _dwconv_ln_swish(x, dw, dw_b, ln_g, ln_b, tm=256):
    bsz, s, d = x.shape
    tm = min(tm, s)
    per = tm // _HALO
    return pl.pallas_call(
        functools.partial(_dwconv_kernel, tm=tm),
        out_shape=jax.ShapeDtypeStruct((bsz, s, d), BF16),
        grid=(bsz, s // tm),
        in_specs=[pl.BlockSpec((1, tm, d), lambda b, i: (b, i, 0)),
                  pl.BlockSpec((1, _HALO, d), lambda b, i: (b, jnp.maximum(i * per - 1, 0), 0)),
                  pl.BlockSpec((_HALO, d), lambda b, i: (0, 0)),
                  pl.BlockSpec((1, d), lambda b, i: (0, 0)),
                  pl.BlockSpec((1, d), lambda b, i: (0, 0)),
                  pl.BlockSpec((1, d), lambda b, i: (0, 0))],
        out_specs=pl.BlockSpec((1, tm, d), lambda b, i: (b, i, 0)),
        scratch_shapes=[pltpu.VMEM((_HALO + tm, d), F32),
                        pltpu.VMEM((_SUBLANES - 1, tm + _SUBLANES, d), F32),
                        pltpu.VMEM((tm, d), F32)],
        compiler_params=_cparams(2),
        name="dwconv_ln_swish",
    )(x, x, dw, dw_b, ln_g, ln_b)


def _rope_tables(positions):
    half = ROPE_DIM // 2
    inv_freq = ROPE_THETA ** (-jnp.arange(half, dtype=F32) / half)
    zeros = jnp.zeros((HEAD_DIM - ROPE_DIM,), F32)
    reps = LANES // HEAD_DIM
    freq = jnp.tile(jnp.concatenate([inv_freq, inv_freq, zeros]), reps)
    sign = jnp.tile(jnp.concatenate([-jnp.ones((half,), F32), jnp.ones((half,), F32), zeros]), reps)
    ang = positions.astype(F32)[..., None] * freq
    return jnp.cos(ang), jnp.sin(ang) * sign


def _nsa_weight(w_in, d):
    kv_end = d + 6 * NSA_KV_GROUPS * HEAD_DIM
    wg = w_in[:, kv_end:].reshape(d, NSA_KV_GROUPS, NSA_Q_PER_GROUP, 3)
    wg = wg.transpose(0, 1, 3, 2).reshape(d, NSA_KV_GROUPS, 3 * NSA_Q_PER_GROUP)
    wg = jnp.pad(wg, ((0, 0), (0, 0), (0, LANES - 3 * NSA_Q_PER_GROUP)))
    return jnp.concatenate([w_in[:, :kv_end], wg.reshape(d, NSA_KV_GROUPS * LANES)], axis=1).astype(BF16)


def kernel(x, c, positions, ada_w, ada_b, mix_pre_g, mix_post_g, ffn_pre_g, ffn_post_g, ffn_w1, ffn_w2, sb_w_in, sb_w_out, nsa_w_in, nsa_w_out, nsa_pe_k, nsa_w1_k, nsa_w2_k, nsa_pe_v, nsa_w1_v, nsa_w2_v, cv_w_in, cv_b_in, cv_dw, cv_dw_b, cv_ln_g, cv_ln_b, cv_w_out, cv_b_out):
    bsz, s, d = x.shape
    depth = ada_w.shape[0]
    n_mixers = 3
    mod = _ada_mod(c, ada_w, ada_b).reshape(depth, bsz, 6, 1, d)
    zero_bias = jnp.zeros((1, d), F32)
    h = x
    for i in range(depth):
        sh1, sc1, g1, sh2, sc2, g2 = [mod[i, :, m] for m in range(6)]
        pre_g = mix_pre_g[i].reshape(1, d)
        post_g = mix_post_g[i].reshape(1, d)
        kind, j = i % n_mixers, i // n_mixers
        if kind == 0:
            qkv = _sb_proj(h, pre_g, sc1, sh1, sb_w_in[j].astype(BF16))
            a = _sb_attention(qkv)
            w_out, b_out = sb_w_out[j], zero_bias
        elif kind == 1:
            cos_t, sin_t = _rope_tables(positions)
            q, kvc, kk, vv, gates = _nsa_proj(h, pre_g, sc1, sh1, _nsa_weight(nsa_w_in[j], d), cos_t, sin_t)
            end = jnp.minimum(jnp.arange(s // CMP_STRIDE) * CMP_STRIDE + CMP_LEN - 1, s - 1)
            cos_c, sin_c = _rope_tables(positions[:, end])
            k_cmp = _compress(kvc, 0, nsa_pe_k[j], nsa_w1_k[j], nsa_w2_k[j], cos_c, sin_c, True)
            v_cmp = _compress(kvc, 1, nsa_pe_v[j], nsa_w1_v[j], nsa_w2_v[j], cos_c, sin_c, False)
            a = _nsa_attention(q, kk, vv, k_cmp, v_cmp, gates)
            w_out, b_out = nsa_w_out[j], zero_bias
        else:
            a = _cv_proj(h, pre_g, sc1, sh1, cv_w_in[j].astype(BF16), cv_b_in[j].reshape(1, 2 * d))
            dw = jnp.pad(cv_dw[j].reshape(CONV_WIDTH, d), ((0, _HALO - CONV_WIDTH), (0, 0)))
            a = _dwconv_ln_swish(a, dw, cv_dw_b[j].reshape(1, d), cv_ln_g[j].reshape(1, d),
                                 cv_ln_b[j].reshape(1, d))
            w_out, b_out = cv_w_out[j], cv_b_out[j].reshape(1, d)
        h = _out_mlp(a, w_out.astype(BF16), b_out, h, g1, post_g, ffn_pre_g[i].reshape(1, d), sc2, sh2,
                     ffn_w1[i].astype(BF16), ffn_w2[i].astype(BF16), g2, ffn_post_g[i].reshape(1, d))
    return h
```

```python
import functools

import numpy as np
import jax
import jax.numpy as jnp
from jax import lax
from jax.experimental import pallas as pl
from jax.experimental.pallas import tpu as pltpu

F32 = jnp.float32
BF16 = jnp.bfloat16

N_HEADS = 16
HEAD_DIM = 64
ROPE_THETA = 500000.0
ROPE_DIM = HEAD_DIM // 4
NORM_EPS = 1e-6
NSA_KV_GROUPS = 4
NSA_Q_PER_GROUP = N_HEADS // NSA_KV_GROUPS
CMP_LEN = 32
CMP_STRIDE = 16
SLC_LEN = 64
N_SELECT = 16
WINDOW = 512
FORCE_BONUS = 1e4
NEG_INF = -1e30
CONV_WIDTH = 31
ATTN_SCALE = HEAD_DIM ** -0.5
LOG2_E = 1.4426950408889634

LANES = 128
_SUBLANES = 8
VMEM_LIMIT_BYTES = 56 * 1024 * 1024

_NT = (((1,), (1,)), ((), ()))


def _cparams(n_axes):
    return pltpu.CompilerParams(dimension_semantics=("arbitrary",) * n_axes,
                                vmem_limit_bytes=VMEM_LIMIT_BYTES)


def _dot(a, b):
    return jnp.dot(a, b, preferred_element_type=F32)


def _dot_nt(a, b):
    return lax.dot_general(a, b, _NT, preferred_element_type=F32)


def _split_bf16(x):
    hi = x.astype(BF16)
    lo = (x - hi.astype(F32)).astype(BF16)
    return hi, lo


def _rms(x):
    return x * lax.rsqrt(jnp.mean(x * x, axis=-1, keepdims=True) + NORM_EPS)


def _norm_mod(h, g, sc, sh):
    return (_rms(h) * g) * (1.0 + sc) + sh


def _rope(x, cos_t, sin_t):
    rows, w = x.shape
    reps = w // LANES
    cos_w = jnp.concatenate([cos_t] * reps, axis=1) if reps > 1 else cos_t
    sin_w = jnp.concatenate([sin_t] * reps, axis=1) if reps > 1 else sin_t
    half = ROPE_DIM // 2
    lane = lax.broadcasted_iota(jnp.int32, (rows, w), 1)
    first_half = (lane & (HEAD_DIM - 1)) < half
    partner = jnp.where(first_half, pltpu.roll(x, w - half, 1), pltpu.roll(x, half, 1))
    return x * cos_w + partner * sin_w


def _ada_kernel(c_ref, w_ref, b_ref, o_ref):
    c = c_ref[...]
    cond = c * jax.nn.sigmoid(c)
    o_ref[0] = jnp.dot(cond, w_ref[0], preferred_element_type=F32,
                       precision=lax.Precision.HIGHEST) + b_ref[0]


def _ada_mod(c, ada_w, ada_b):
    depth, d, n = ada_w.shape
    b = c.shape[0]
    tn = 1024
    return pl.pallas_call(
        _ada_kernel,
        out_shape=jax.ShapeDtypeStruct((depth, b, n), F32),
        grid=(depth, n // tn),
        in_specs=[pl.BlockSpec((b, d), lambda i, j: (0, 0)),
                  pl.BlockSpec((1, d, tn), lambda i, j: (i, 0, j)),
                  pl.BlockSpec((1, 1, tn), lambda i, j: (i, 0, j))],
        out_specs=pl.BlockSpec((1, b, tn), lambda i, j: (i, 0, j)),
        compiler_params=_cparams(2),
        name="ada_mod",
    )(c, ada_w, ada_b.reshape(depth, 1, n))


def _proj_in_specs(tm, d, n_w):
    return [pl.BlockSpec((1, tm, d), lambda b, i: (b, i, 0)),
            pl.BlockSpec((1, d), lambda b, i: (0, 0)),
            pl.BlockSpec((1, 1, d), lambda b, i: (b, 0, 0)),
            pl.BlockSpec((1, 1, d), lambda b, i: (b, 0, 0)),
            pl.BlockSpec((d, n_w), lambda b, i: (0, 0))]


def _sb_proj_kernel(h_ref, g_ref, sc_ref, sh_ref, w_ref, o_ref, *, tn):
    d = h_ref.shape[2]
    u = _norm_mod(h_ref[0], g_ref[...], sc_ref[0], sh_ref[0]).astype(BF16)
    for j in range(w_ref.shape[1] // tn):
        acc = _dot(u, w_ref[:, j * tn:(j + 1) * tn])
        if j * tn < d:
            acc = acc * (-ATTN_SCALE)
        o_ref[0, :, j * tn:(j + 1) * tn] = acc.astype(BF16)


def _sb_proj(h, g, sc, sh, w, tm=512, tn=512):
    bsz, s, d = h.shape
    n = w.shape[1]
    return pl.pallas_call(
        functools.partial(_sb_proj_kernel, tn=tn),
        out_shape=jax.ShapeDtypeStruct((bsz, s, n), BF16),
        grid=(bsz, s // tm),
        in_specs=_proj_in_specs(tm, d, n),
        out_specs=pl.BlockSpec((1, tm, n), lambda b, i: (b, i, 0)),
        compiler_params=_cparams(2),
        name="sb_proj",
    )(h, g, sc, sh, w)


def _nsa_proj_kernel(h_ref, g_ref, sc_ref, sh_ref, w_ref, cos_ref, sin_ref,
                     q_ref, kvc_ref, kk_ref, vv_ref, gate_ref, *, tn):
    d = h_ref.shape[2]
    tm = h_ref.shape[1]
    kv_w = NSA_KV_GROUPS * HEAD_DIM
    u = _norm_mod(h_ref[0], g_ref[...], sc_ref[0], sh_ref[0]).astype(BF16)
    cos_t = cos_ref[0]
    sin_t = sin_ref[0]
    one_col = jnp.where(lax.broadcasted_iota(jnp.int32, (tm, LANES - HEAD_DIM), 1) == 0, 1.0, 0.0)
    n_q = d // tn
    n_kv = 6 * kv_w // tn
    for j in range(w_ref.shape[1] // tn):
        acc = _dot(u, w_ref[:, j * tn:(j + 1) * tn])
        if j < n_q:
            q = _rope(acc, cos_t, sin_t) * ATTN_SCALE
            q_ref[0, :, j * tn:(j + 1) * tn] = q.astype(BF16)
        elif j < n_q + n_kv:
            for part in range(tn // kv_w):
                i = (j - n_q) * (tn // kv_w) + part
                x = acc[:, part * kv_w:(part + 1) * kv_w]
                if i in (2, 4):
                    x = _rope(x, cos_t, sin_t)
                for grp in range(NSA_KV_GROUPS):
                    xg = x[:, grp * HEAD_DIM:(grp + 1) * HEAD_DIM]
                    if i < 2:
                        kvc_ref[0, i, grp] = xg
                    elif i in (2, 4):
                        kk_ref[0, (i - 2) // 2, grp] = xg.astype(BF16)
                    else:
                        vv_ref[0, (i - 3) // 2, grp] = jnp.concatenate([xg, one_col], axis=1).astype(BF16)
        else:
            c0 = (j - n_q - n_kv) * tn
            gate_ref[0, :, c0:c0 + tn] = jax.nn.sigmoid(acc)


def _nsa_proj(h, g, sc, sh, w, cos_t, sin_t, tm=512, tn=256):
    bsz, s, d = h.shape
    n = w.shape[1]
    n_gate = NSA_KV_GROUPS * LANES
    return pl.pallas_call(
        functools.partial(_nsa_proj_kernel, tn=tn),
        out_shape=(jax.ShapeDtypeStruct((bsz, s, d), BF16),
                   jax.ShapeDtypeStruct((bsz, 2, NSA_KV_GROUPS, s, HEAD_DIM), F32),
                   jax.ShapeDtypeStruct((bsz, 2, NSA_KV_GROUPS, s, HEAD_DIM), BF16),
                   jax.ShapeDtypeStruct((bsz, 2, NSA_KV_GROUPS, s, LANES), BF16),
                   jax.ShapeDtypeStruct((bsz, s, n_gate), F32)),
        grid=(bsz, s // tm),
        in_specs=_proj_in_specs(tm, d, n) + [
            pl.BlockSpec((1, tm, LANES), lambda b, i: (b, i, 0)),
            pl.BlockSpec((1, tm, LANES), lambda b, i: (b, i, 0))],
        out_specs=(pl.BlockSpec((1, tm, d), lambda b, i: (b, i, 0)),
                   pl.BlockSpec((1, 2, NSA_KV_GROUPS, tm, HEAD_DIM), lambda b, i: (b, 0, 0, i, 0)),
                   pl.BlockSpec((1, 2, NSA_KV_GROUPS, tm, HEAD_DIM), lambda b, i: (b, 0, 0, i, 0)),
                   pl.BlockSpec((1, 2, NSA_KV_GROUPS, tm, LANES), lambda b, i: (b, 0, 0, i, 0)),
                   pl.BlockSpec((1, tm, n_gate), lambda b, i: (b, i, 0))),
        compiler_params=_cparams(2),
        name="nsa_proj",
    )(h, g, sc, sh, w, cos_t, sin_t)


def _cv_proj_kernel(h_ref, g_ref, sc_ref, sh_ref, w_ref, b_ref, o_ref, *, tn):
    d = h_ref.shape[2]
    u = _norm_mod(h_ref[0], g_ref[...], sc_ref[0], sh_ref[0]).astype(BF16)
    for j in range(d // tn):
        a = _dot(u, w_ref[:, j * tn:(j + 1) * tn]) + b_ref[:, j * tn:(j + 1) * tn]
        gt = _dot(u, w_ref[:, d + j * tn:d + (j + 1) * tn]) + b_ref[:, d + j * tn:d + (j + 1) * tn]
        o_ref[0, :, j * tn:(j + 1) * tn] = a * jax.nn.sigmoid(gt)


def _cv_proj(h, g, sc, sh, w, bias, tm=512, tn=256):
    bsz, s, d = h.shape
    n = w.shape[1]
    return pl.pallas_call(
        functools.partial(_cv_proj_kernel, tn=tn),
        out_shape=jax.ShapeDtypeStruct((bsz, s, d), F32),
        grid=(bsz, s // tm),
        in_specs=_proj_in_specs(tm, d, n) + [pl.BlockSpec((1, n), lambda b, i: (0, 0))],
        out_specs=pl.BlockSpec((1, tm, d), lambda b, i: (b, i, 0)),
        compiler_params=_cparams(2),
        name="cv_proj",
    )(h, g, sc, sh, w, bias)


def _out_mlp_kernel(a_ref, wo_ref, bo_ref, h_ref, g1_ref, pg1_ref, pre_ref, sc_ref, sh_ref,
                    w1_ref, w2_ref, g2_ref, pg2_ref, o_ref, h1_ref, hid_ref, *, tf, tn):
    a = a_ref[0]
    d = wo_ref.shape[1]
    d_ff = w1_ref.shape[1]
    for j in range(d // tn):
        o_ref[0, :, j * tn:(j + 1) * tn] = (_dot(a, wo_ref[:, j * tn:(j + 1) * tn])
                                            + bo_ref[:, j * tn:(j + 1) * tn])
    h1_ref[...] = h_ref[0] + g1_ref[0] * (_rms(o_ref[0]) * pg1_ref[...])
    u = _norm_mod(h1_ref[...], pre_ref[...], sc_ref[0], sh_ref[0]).astype(BF16)
    for c in range(d_ff // tf):
        act = jnp.maximum(_dot(u, w1_ref[:, c * tf:(c + 1) * tf]), 0.0)
        hid_ref[:, c * tf:(c + 1) * tf] = (act * act).astype(BF16)
    for j in range(d // tn):
        o_ref[0, :, j * tn:(j + 1) * tn] = _dot(hid_ref[...], w2_ref[:, j * tn:(j + 1) * tn])
    o_ref[0] = h1_ref[...] + g2_ref[0] * (_rms(o_ref[0]) * pg2_ref[...])


def _out_mlp(a, w_out, b_out, h, g1, post_g1, pre_g2, sc2, sh2, w1, w2, g2, post_g2,
             tm=512, tf=512, tn=256):
    bsz, s, k = a.shape
    d = w_out.shape[1]
    d_ff = w1.shape[1]

    def rows(width):
        return pl.BlockSpec((1, tm, width), lambda b, i: (b, i, 0))

    def per_batch():
        return pl.BlockSpec((1, 1, d), lambda b, i: (b, 0, 0))

    def whole(shape, single=False):
        return pl.BlockSpec(shape, lambda b, i: (0, 0), pipeline_mode=pl.Buffered(1) if single else None)

    return pl.pallas_call(
        functools.partial(_out_mlp_kernel, tf=tf, tn=tn),
        out_shape=jax.ShapeDtypeStruct((bsz, s, d), F32),
        grid=(bsz, s // tm),
        in_specs=[rows(k), whole((k, d), True), whole((1, d)), rows(d), per_batch(), whole((1, d)),
                  whole((1, d)), per_batch(), per_batch(),
                  whole((d, d_ff), True), whole((d_ff, d), True), per_batch(), whole((1, d))],
        out_specs=rows(d),
        scratch_shapes=[pltpu.VMEM((tm, d), F32), pltpu.VMEM((tm, d_ff), BF16)],
        compiler_params=_cparams(2),
        name="out_mlp",
    )(a, w_out, b_out, h, g1, post_g1, pre_g2, sc2, sh2, w1, w2, g2, post_g2)


SB_ZERO_WEIGHT_LOG = -110.0


def _sb_tiles(qs, ks, vs, tri2, state, strict):
    n = len(qs)
    ws = [_dot_nt(qs[h], ks[h]) for h in range(n)]
    lks, hls = [], []
    for w in ws:
        lk = jnp.minimum(w, 0.0) - jnp.log(1.0 + jnp.exp2(jnp.abs(w) * (-LOG2_E)))
        if strict is not None:
            lk = jnp.where(strict, lk, 0.0)
        hi, lo = _split_bf16(lk)
        lks.append(lk)
        hls.append(jnp.concatenate([hi, lo], axis=1))
    tails = [_dot(hl, tri2) for hl in hls]
    probs = []
    for h in range(n):
        a = jnp.exp((tails[h] - ws[h]) + state[2 * h])
        if strict is not None:
            a = jnp.where(strict, a, 0.0)
        probs.append(a.astype(BF16))
    out = ()
    for h in range(n):
        out += (state[2 * h] + jnp.sum(lks[h], axis=-1, keepdims=True),
                state[2 * h + 1] + _dot(probs[h], vs[h]))
    return out


def _sb_attn_kernel(q_ref, k_ref, v_ref, tri_ref, o_ref, *, t, pairs):
    qi = pl.program_id(2)
    tri2 = tri_ref[...]
    lane = lax.broadcasted_iota(jnp.int32, (t, LANES), 1)
    row = lax.broadcasted_iota(jnp.int32, (t, t), 0)
    col = lax.broadcasted_iota(jnp.int32, (t, t), 1)
    strict = col < row
    q_heads = []
    for p in range(pairs):
        q = q_ref[0, :, p * LANES:(p + 1) * LANES]
        zero_q = jnp.zeros_like(q)
        q_heads += [jnp.where(lane < HEAD_DIM, q, zero_q),
                    jnp.where(lane >= HEAD_DIM, q, zero_q)]
    n_heads = 2 * pairs

    def tiles(kj, state, mask):
        start = pl.multiple_of(kj * t, t)
        cols = [slice((hd // 2) * LANES, (hd // 2 + 1) * LANES) for hd in range(n_heads)]
        ks = [k_ref[0, pl.ds(start, t), c] for c in cols]
        vs = [v_ref[0, pl.ds(start, t), c] for c in cols]
        return _sb_tiles(q_heads, ks, vs, tri2, state, mask)

    def largest(state):
        carries = state[0::2]
        top = carries[0]
        for c in carries[1:]:
            top = jnp.maximum(top, c)
        return jnp.max(top)

    zeros = (jnp.zeros((t, 1), F32), jnp.zeros((t, LANES), F32))
    state = tiles(qi, zeros * n_heads, strict)

    def cond(loop):
        return (loop[0] < qi) & (loop[1] > SB_ZERO_WEIGHT_LOG)

    def body(loop):
        state = tiles(qi - 1 - loop[0], loop[2:], None)
        return (loop[0] + 1, largest(state)) + state

    out = lax.while_loop(cond, body, (jnp.int32(0), largest(state)) + state)[2:]
    for p in range(pairs):
        o_ref[0, :, p * LANES:(p + 1) * LANES] = jnp.where(
            lane < HEAD_DIM, out[4 * p + 1], out[4 * p + 3]).astype(BF16)


def _sb_attention(qkv, t=256, pairs=2):
    bsz, s, n3 = qkv.shape
    d = n3 // 3
    w = pairs * LANES
    n_grp = d // w
    t = min(t, s)
    tri = np.tril(np.ones((t, t), np.float32))
    tri = jnp.asarray(np.concatenate([tri, tri], axis=0), BF16)
    return pl.pallas_call(
        functools.partial(_sb_attn_kernel, t=t, pairs=pairs),
        out_shape=jax.ShapeDtypeStruct((bsz, s, d), BF16),
        grid=(bsz, n_grp, s // t),
        in_specs=[pl.BlockSpec((1, t, w), lambda b, p, i: (b, i, p)),
                  pl.BlockSpec((1, s, w), lambda b, p, i: (b, 0, n_grp + p)),
                  pl.BlockSpec((1, s, w), lambda b, p, i: (b, 0, 2 * n_grp + p)),
                  pl.BlockSpec((2 * t, t), lambda b, p, i: (0, 0))],
        out_specs=pl.BlockSpec((1, t, w), lambda b, p, i: (b, i, p)),
        compiler_params=_cparams(3),
        name="sb_attention",
    )(qkv, qkv, qkv, tri)


def _compress_kernel(x_ref, pe_ref, w1_ref, w2_ref, cos_ref, sin_ref, o_ref, *, rope):
    w1 = w1_ref[...]
    hid = w1.shape[1] // 2
    n_seg = o_ref.shape[2]
    pre = jnp.zeros((n_seg, 2 * hid), F32)
    for tok in range(CMP_STRIDE):
        x_tok = x_ref[0, 0, 0, pl.ds(tok, n_seg, stride=CMP_STRIDE), :].astype(BF16)
        pre = pre + _dot(x_tok, w1[tok * HEAD_DIM:(tok + 1) * HEAD_DIM, :])
    pe_term = _dot(pe_ref[...], w1)
    bias = pe_term[0:1, :hid] + pe_term[8:9, hid:]
    nxt = pltpu.roll(pre[:, hid:], n_seg - 1, 0)
    mid = jax.nn.gelu(pre[:, :hid] + nxt + bias)
    out = _dot(mid.astype(BF16), w2_ref[...])
    if rope:
        out = _rope(out, cos_ref[0], sin_ref[0])
    o_ref[0, 0] = out[:, :HEAD_DIM].astype(BF16)


def _compress(kvc, which, pe, w1, w2, cos_c, sin_c, rope):
    bsz, _, grp, s, _ = kvc.shape
    n_seg = s // CMP_STRIDE
    hid = w1.shape[1]
    half = w1.shape[0] // 2
    w1cat = jnp.concatenate([w1[:half], w1[half:]], axis=1).astype(BF16)
    w2p = jnp.pad(w2, ((0, 0), (0, LANES - HEAD_DIM))).astype(BF16)
    pe_flat = pe.reshape(2, half)
    pe_rows = jnp.zeros((16, half), F32).at[0].set(pe_flat[0]).at[8].set(pe_flat[1]).astype(BF16)
    return pl.pallas_call(
        functools.partial(_compress_kernel, rope=rope),
        out_shape=jax.ShapeDtypeStruct((bsz, grp, n_seg, HEAD_DIM), BF16),
        grid=(bsz, grp),
        in_specs=[pl.BlockSpec((1, 1, 1, s, HEAD_DIM), lambda b, g: (b, which, g, 0, 0)),
                  pl.BlockSpec((16, half), lambda b, g: (0, 0)),
                  pl.BlockSpec((half, 2 * hid), lambda b, g: (0, 0)),
                  pl.BlockSpec((hid, LANES), lambda b, g: (0, 0)),
                  pl.BlockSpec((1, n_seg, LANES), lambda b, g: (b, 0, 0)),
                  pl.BlockSpec((1, n_seg, LANES), lambda b, g: (b, 0, 0))],
        out_specs=pl.BlockSpec((1, 1, n_seg, HEAD_DIM), lambda b, g: (b, g, 0, 0)),
        compiler_params=_cparams(2),
        name="nsa_compress",
    )(kvc, pe_rows, w1cat, w2p, cos_c, sin_c)


def _masked_scores(q4, k, madd, r_heads):
    rows, tk = q4.shape[0], k.shape[0]
    s = _dot_nt(q4, k).reshape(r_heads, rows // r_heads, tk) + madd[None]
    return s.reshape(rows, tk)


def _online_softmax(scores, values, states):
    stats = []
    for s, (m, _) in zip(scores, states):
        m_new = jnp.maximum(m, jnp.max(s, axis=-1, keepdims=True))
        stats.append((m_new, jnp.exp(m - m_new), jnp.exp(s - m_new).astype(BF16)))
    pv = [_dot(st[2], v) for st, v in zip(stats, values)]
    return tuple((st[0], st[1] * state[1] + o) for st, state, o in zip(stats, states, pv))


def _softmax_result(state):
    acc = state[1]
    return acc[:, :HEAD_DIM] / acc[:, HEAD_DIM:HEAD_DIM + 1]


def _nsa_attn_kernel(q_ref, kc_ref, vc_ref, ks_ref, vs_ref, kw_ref, vw_ref, gate_ref,
                     ovt_ref, eye_ref, exp_ref, o_ref, *, tq, tk, wk, n_cmp, n_sel, ng):
    qi = pl.program_id(2)
    r_heads = NSA_Q_PER_GROUP
    rows = r_heads * tq
    gw = r_heads * HEAD_DIM
    t0 = qi * tq
    n_cp = kc_ref.shape[2]
    n_slc = ovt_ref.shape[0]
    groups = range(ng)

    q4 = []
    for g in groups:
        qf = q_ref[0, :, g * gw:(g + 1) * gw].astype(F32)
        q4.append(jnp.concatenate([qf[:, r * HEAD_DIM:(r + 1) * HEAD_DIM] for r in range(r_heads)],
                                  axis=0).astype(BF16))

    t_c = t0 + lax.broadcasted_iota(jnp.int32, (tq, n_cp), 0)
    n_c = lax.broadcasted_iota(jnp.int32, (tq, n_cp), 1)
    cmp_ok = (n_c * CMP_STRIDE + (CMP_LEN - 1) <= t_c) & (n_c < n_cmp)
    any_ok = (t_c[:, 0:1] >= CMP_LEN - 1).astype(F32)
    sc = [_dot_nt(q4[g], kc_ref[0, g]).reshape(r_heads, tq, n_cp) for g in groups]
    p_cmp = []
    for g in groups:
        s = jnp.where(cmp_ok[None], sc[g], NEG_INF)
        e = jnp.exp(s - jnp.max(s, axis=-1, keepdims=True))
        p_cmp.append(e * (any_ok[None] / jnp.sum(e, axis=-1, keepdims=True)))
    o_cmp = [_dot(p_cmp[g].reshape(rows, n_cp).astype(BF16), vc_ref[0, g]) for g in groups]

    ovt = ovt_ref[...]
    imp_t = []
    for g in groups:
        p_hi, p_lo = _split_bf16(jnp.sum(p_cmp[g], axis=0))
        imp_t.append(_dot_nt(ovt, p_hi) + _dot_nt(ovt, p_lo))
    blk = lax.broadcasted_iota(jnp.int32, (n_slc, tq), 0)
    tok = t0 + lax.broadcasted_iota(jnp.int32, (n_slc, tq), 1)
    cur = lax.shift_right_logical(tok, SLC_LEN.bit_length() - 1)
    forced = (blk == 0) | (blk == cur) | (blk == cur - 1)
    causal_blk = blk * SLC_LEN <= tok
    neg_t = []
    for g in groups:
        score = jnp.where(causal_blk, jnp.where(forced, FORCE_BONUS, imp_t[g]), NEG_INF)
        slabs = [score[r0:r0 + _SUBLANES] for r0 in range(0, n_slc, _SUBLANES)]
        ranks = [jnp.zeros((_SUBLANES, tq), jnp.int32) for _ in slabs]
        for i in range(n_slc):
            s_i = score[i:i + 1, :]
            for v, slab in enumerate(slabs):
                r0 = v * _SUBLANES
                if r0 > i:
                    ahead = s_i >= slab
                elif r0 + _SUBLANES - 1 <= i:
                    ahead = s_i > slab
                else:
                    ahead = (s_i > slab) | ((s_i == slab) & (blk[:_SUBLANES] > i - r0))
                ranks[v] = ranks[v] + jnp.where(ahead, 1, 0)
        rank = jnp.concatenate(ranks, axis=0)
        neg_t.append(jnp.where(rank < n_sel, 0.0, NEG_INF).astype(BF16))
    eye = eye_ref[...]
    neg = [_dot_nt(eye, neg_t[g]).astype(BF16) for g in groups]

    def slc_block(kj, n_blk):
        width = n_blk * tk
        start = pl.multiple_of(kj * tk, tk)
        expand = exp_ref[kj] if n_blk == 1 else jnp.concatenate([exp_ref[kj + i] for i in range(n_blk)], axis=1)
        causal = (lax.broadcasted_iota(jnp.int32, (tq, width), 1) + start
                  <= t0 + lax.broadcasted_iota(jnp.int32, (tq, width), 0))
        madd = [jnp.where(causal, _dot(neg[g], expand), NEG_INF) for g in groups]
        scores = [_masked_scores(q4[g], ks_ref[0, g, pl.ds(start, width), :], madd[g], r_heads) for g in groups]
        return scores, [vs_ref[0, g, pl.ds(start, width), :] for g in groups]

    last = (t0 + tq - 1) // tk
    init = (jnp.full((rows, 1), NEG_INF, F32), jnp.zeros((rows, LANES), F32))
    states = lax.fori_loop(0, lax.shift_right_logical(last, 1),
                           lambda j, st: _online_softmax(*slc_block(2 * j, 2), st), (init,) * ng)
    states = lax.cond((last & 1) == 1,
                      lambda st: _online_softmax(*slc_block(jnp.maximum(last - 1, 0), 1), st),
                      lambda st: st, states)

    start_w = pl.multiple_of(jnp.maximum(t0 + tq - wk, 0), tq)
    t_w = t0 + lax.broadcasted_iota(jnp.int32, (tq, wk), 0)
    key_w = start_w + lax.broadcasted_iota(jnp.int32, (tq, wk), 1)
    madd_w = jnp.where((key_w <= t_w) & (key_w > t_w - WINDOW), 0.0, NEG_INF)
    s_win = [_masked_scores(q4[g], kw_ref[0, g, pl.ds(start_w, wk), :], madd_w, r_heads) for g in groups]
    v_win = [vw_ref[0, g, pl.ds(start_w, wk), :] for g in groups]
    s_last, v_last = slc_block(last, 1)
    done = _online_softmax(s_last + s_win, v_last + v_win, states + (init,) * ng)

    for g in groups:
        o_slc = _softmax_result(done[g])
        o_win = _softmax_result(done[ng + g])
        gates = gate_ref[0, :, g * LANES:(g + 1) * LANES]
        pieces = []
        for r in range(r_heads):
            sl = slice(r * tq, (r + 1) * tq)
            pieces.append(gates[:, r:r + 1] * o_cmp[g][sl]
                          + gates[:, r_heads + r:r_heads + r + 1] * o_slc[sl]
                          + gates[:, 2 * r_heads + r:2 * r_heads + r + 1] * o_win[sl])
        o_ref[0, :, g * gw:(g + 1) * gw] = jnp.concatenate(pieces, axis=1).astype(BF16)


def _nsa_attention(q, kk, vv, k_cmp, v_cmp, gates, tq=128, tk=512, ng=4):
    bsz, s, d = q.shape
    grp = NSA_KV_GROUPS
    gw = NSA_Q_PER_GROUP * HEAD_DIM
    n_cp = k_cmp.shape[2]
    n_cmp = (s - CMP_LEN) // CMP_STRIDE + 1
    n_slc = s // SLC_LEN
    n_sel = min(N_SELECT, n_slc)
    tq = min(tq, s)
    tk = min(tk, s)
    c0 = np.arange(n_cp)[None, :] * CMP_STRIDE
    s0 = np.arange(n_slc)[:, None] * SLC_LEN
    ov = np.maximum(np.minimum(c0 + CMP_LEN, s0 + SLC_LEN) - np.maximum(c0, s0), 0) / CMP_LEN
    ov = ov * (np.arange(n_cp)[None, :] < n_cmp)
    ovt = jnp.asarray(ov, BF16)
    eye = jnp.asarray(np.eye(tq, dtype=np.float32), BF16)
    n_kt = s // tk
    expand = (np.arange(n_slc)[None, :, None]
              == (np.arange(n_kt)[:, None, None] * (tk // SLC_LEN) + np.arange(tk)[None, None, :] // SLC_LEN))
    expand = jnp.asarray(expand.astype(np.float32), BF16)

    wk = min(-(-(WINDOW + tq - 1) // tq) * tq, s)

    def kv_spec(i, width):
        return pl.BlockSpec((1, 1, ng, s, width), lambda b, g, t, i=i: (b, i, g, 0, 0),
                            pipeline_mode=pl.Buffered(1))

    def kernel(q_ref, kc_ref, vc_ref, ks_ref, vs_ref, kw_ref, vw_ref, *rest):
        return _nsa_attn_kernel(q_ref, kc_ref, vc_ref, ks_ref.at[0], vs_ref.at[0], kw_ref.at[0],
                                vw_ref.at[0], *rest, tq=tq, tk=tk, wk=wk, n_cmp=n_cmp, n_sel=n_sel, ng=ng)

    return pl.pallas_call(
        kernel,
        out_shape=jax.ShapeDtypeStruct((bsz, s, d), BF16),
        grid=(bsz, grp // ng, s // tq),
        in_specs=[pl.BlockSpec((1, tq, ng * gw), lambda b, g, t: (b, t, g)),
                  pl.BlockSpec((1, ng, n_cp, HEAD_DIM), lambda b, g, t: (b, g, 0, 0)),
                  pl.BlockSpec((1, ng, n_cp, HEAD_DIM), lambda b, g, t: (b, g, 0, 0)),
                  kv_spec(0, HEAD_DIM), kv_spec(0, LANES), kv_spec(1, HEAD_DIM), kv_spec(1, LANES),
                  pl.BlockSpec((1, tq, ng * LANES), lambda b, g, t: (b, t, g)),
                  pl.BlockSpec((n_slc, n_cp), lambda b, g, t: (0, 0)),
                  pl.BlockSpec((tq, tq), lambda b, g, t: (0, 0)),
                  pl.BlockSpec((n_kt, n_slc, tk), lambda b, g, t: (0, 0, 0))],
        out_specs=pl.BlockSpec((1, tq, ng * gw), lambda b, g, t: (b, t, g)),
        compiler_params=_cparams(3),
        name="nsa_attention",
    )(q, k_cmp, v_cmp, kk, vv, kk, vv, gates, ovt, eye, expand)


_HALO = 32


_CONV_ROWS = 64


def _dwconv_kernel(x_ref, prev_ref, dw_ref, dwb_ref, lng_ref, lnb_ref, o_ref, ext_ref, z_ref, y_ref,
                   *, tm):
    i = pl.program_id(1)
    d = x_ref.shape[2]
    prev = prev_ref[0]
    ext_ref[0:_HALO, :] = jnp.where(i > 0, prev, jnp.zeros_like(prev))
    ext_ref[_HALO:_HALO + tm, :] = x_ref[0]
    lead = _HALO - (CONV_WIDTH - 1)
    offsets = range(lead, lead + CONV_WIDTH)
    rb = _CONV_ROWS
    for c0 in range(0, d, LANES):
        cols = slice(c0, c0 + LANES)

        def tap(r0, n, j, shift):
            return ext_ref[r0 + j - shift:r0 + j - shift + n, cols] * dw_ref[j - lead:j - lead + 1, cols]

        for s in range(1, _SUBLANES):
            group = [j for j in offsets if j % _SUBLANES == s]
            for r0 in range(0, tm + _SUBLANES, rb):
                n = min(rb, tm + _SUBLANES - r0)
                acc = tap(r0, n, group[0], s)
                for j in group[1:]:
                    acc = acc + tap(r0, n, j, s)
                z_ref[s - 1, r0:r0 + n, cols] = acc
        for r0 in range(0, tm, rb):
            acc = jnp.zeros((rb, LANES), F32) + dwb_ref[:, cols]
            for j in offsets:
                if j % _SUBLANES == 0:
                    acc = acc + tap(r0, rb, j, 0)
            for s in range(1, _SUBLANES):
                acc = acc + z_ref[s - 1, r0 + s:r0 + s + rb, cols]
            y_ref[r0:r0 + rb, cols] = acc
    acc = y_ref[...]
    mu = jnp.mean(acc, axis=-1, keepdims=True)
    cen = acc - mu
    var = jnp.mean(cen * cen, axis=-1, keepdims=True)
    y = cen * lax.rsqrt(var + NORM_EPS) * lng_ref[...] + lnb_ref[...]
    o_ref[0] = (y * jax.nn.sigmoid(y)).astype(BF16)


def _dwconv_ln_swish(x, dw, dw_b, ln_g, ln_b, tm=256):
    bsz, s, d = x.shape
    tm = min(tm, s)
    per = tm // _HALO
    return pl.pallas_call(
        functools.partial(_dwconv_kernel, tm=tm),
        out_shape=jax.ShapeDtypeStruct((bsz, s, d), BF16),
        grid=(bsz, s // tm),
        in_specs=[pl.BlockSpec((1, tm, d), lambda b, i: (b, i, 0)),
                  pl.BlockSpec((1, _HALO, d), lambda b, i: (b, jnp.maximum(i * per - 1, 0), 0)),
                  pl.BlockSpec((_HALO, d), lambda b, i: (0, 0)),
                  pl.BlockSpec((1, d), lambda b, i: (0, 0)),
                  pl.BlockSpec((1, d), lambda b, i: (0, 0)),
                  pl.BlockSpec((1, d), lambda b, i: (0, 0))],
        out_specs=pl.BlockSpec((1, tm, d), lambda b, i: (b, i, 0)),
        scratch_shapes=[pltpu.VMEM((_HALO + tm, d), F32),
                        pltpu.VMEM((_SUBLANES - 1, tm + _SUBLANES, d), F32),
                        pltpu.VMEM((tm, d), F32)],
        compiler_params=_cparams(2),
        name="dwconv_ln_swish",
    )(x, x, dw, dw_b, ln_g, ln_b)


def _rope_tables(positions):
    half = ROPE_DIM // 2
    inv_freq = ROPE_THETA ** (-jnp.arange(half, dtype=F32) / half)
    zeros = jnp.zeros((HEAD_DIM - ROPE_DIM,), F32)
    reps = LANES // HEAD_DIM
    freq = jnp.tile(jnp.concatenate([inv_freq, inv_freq, zeros]), reps)
    sign = jnp.tile(jnp.concatenate([-jnp.ones((half,), F32), jnp.ones((half,), F32), zeros]), reps)
    ang = positions.astype(F32)[..., None] * freq
    return jnp.cos(ang), jnp.sin(ang) * sign


def _nsa_weight(w_in, d):
    kv_end = d + 6 * NSA_KV_GROUPS * HEAD_DIM
    wg = w_in[:, kv_end:].reshape(d, NSA_KV_GROUPS, NSA_Q_PER_GROUP, 3)
    wg = wg.transpose(0, 1, 3, 2).reshape(d, NSA_KV_GROUPS, 3 * NSA_Q_PER_GROUP)
    wg = jnp.pad(wg, ((0, 0), (0, 0), (0, LANES - 3 * NSA_Q_PER_GROUP)))
    return jnp.concatenate([w_in[:, :kv_end], wg.reshape(d, NSA_KV_GROUPS * LANES)], axis=1).astype(BF16)


def kernel(x, c, positions, ada_w, ada_b, mix_pre_g, mix_post_g, ffn_pre_g, ffn_post_g, ffn_w1, ffn_w2, sb_w_in, sb_w_out, nsa_w_in, nsa_w_out, nsa_pe_k, nsa_w1_k, nsa_w2_k, nsa_pe_v, nsa_w1_v, nsa_w2_v, cv_w_in, cv_b_in, cv_dw, cv_dw_b, cv_ln_g, cv_ln_b, cv_w_out, cv_b_out):
    bsz, s, d = x.shape
    depth = ada_w.shape[0]
    n_mixers = 3
    mod = _ada_mod(c, ada_w, ada_b).reshape(depth, bsz, 6, 1, d)
    zero_bias = jnp.zeros((1, d), F32)
    h = x
    for i in range(depth):
        sh1, sc1, g1, sh2, sc2, g2 = [mod[i, :, m] for m in range(6)]
        pre_g = mix_pre_g[i].reshape(1, d)
        post_g = mix_post_g[i].reshape(1, d)
        kind, j = i % n_mixers, i // n_mixers
        if kind == 0:
            qkv = _sb_proj(h, pre_g, sc1, sh1, sb_w_in[j].astype(BF16))
            a = _sb_attention(qkv)
            w_out, b_out = sb_w_out[j], zero_bias
        elif kind == 1:
            cos_t, sin_t = _rope_tables(positions)
            q, kvc, kk, vv, gates = _nsa_proj(h, pre_g, sc1, sh1, _nsa_weight(nsa_w_in[j], d), cos_t, sin_t)
            end = jnp.minimum(jnp.arange(s // CMP_STRIDE) * CMP_STRIDE + CMP_LEN - 1, s - 1)
            cos_c, sin_c = _rope_tables(positions[:, end])
            k_cmp = _compress(kvc, 0, nsa_pe_k[j], nsa_w1_k[j], nsa_w2_k[j], cos_c, sin_c, True)
            v_cmp = _compress(kvc, 1, nsa_pe_v[j], nsa_w1_v[j], nsa_w2_v[j], cos_c, sin_c, False)
            a = _nsa_attention(q, kk, vv, k_cmp, v_cmp, gates)
            w_out, b_out = nsa_w_out[j], zero_bias
        else:
            a = _cv_proj(h, pre_g, sc1, sh1, cv_w_in[j].astype(BF16), cv_b_in[j].reshape(1, 2 * d))
            dw = jnp.pad(cv_dw[j].reshape(CONV_WIDTH, d), ((0, _HALO - CONV_WIDTH), (0, 0)))
            a = _dwconv_ln_swish(a, dw, cv_dw_b[j].reshape(1, d), cv_ln_g[j].reshape(1, d),
                                 cv_ln_b[j].reshape(1, d))
            w_out, b_out = cv_w_out[j], cv_b_out[j].reshape(1, d)
        h = _out_mlp(a, w_out.astype(BF16), b_out, h, g1, post_g, ffn_pre_g[i].reshape(1, d), sc2, sh2,
                     ffn_w1[i].astype(BF16), ffn_w2[i].astype(BF16), g2, ffn_post_g[i].reshape(1, d))
    return h
```
